```python
import jax, jax.numpy as jnp
from jax import lax
import numpy as np

D_MODEL = 1024
BATCH = 16
SEQ = 256
DEPTH = 2
DEC_BATCH = 8
DEC_SEQ = 1024
PAST_LEN = 512

GRID_W = 64
N_EVEN = (DEPTH + 1) // 2
N_ODD = DEPTH // 2
D_MIX = D_MODEL
D_FF = 4 * D_MODEL
EPS = 1e-6

GLA_HEADS = 4
GLA_DK_HEAD = (D_MODEL // 2) // GLA_HEADS
GLA_DV_HEAD = (3 * D_MODEL // 4) // GLA_HEADS
GLA_DK = GLA_HEADS * GLA_DK_HEAD
GLA_DV = GLA_HEADS * GLA_DV_HEAD
GLA_LOWRANK = 16
GLA_TAU = 16.0
GLA_CHUNK = 64
FNET_WIDTH = D_MIX - GLA_DV
FNET_GROUPS = 4
FNET_GROUP_DIM = FNET_WIDTH // FNET_GROUPS
EV_SIZES = (GLA_DK, GLA_DK, GLA_DV, GLA_DV, GLA_LOWRANK, GLA_LOWRANK, FNET_WIDTH)
EV_IN = sum(EV_SIZES)
GMLP_WIDTH = D_MIX // 2
GMLP_GROUPS = 4
GMLP_GROUP_DIM = GMLP_WIDTH // GMLP_GROUPS
GMLP_CHUNK = 128
CONV_WIDTH = D_MIX - GMLP_WIDTH
CONV_K = 3
OD_SIZES = (2 * GMLP_WIDTH, CONV_WIDTH, CONV_WIDTH, CONV_WIDTH)
OD_IN = sum(OD_SIZES)
N_MOD = 6

kernel_name = "hybrid_diffusion_gla_fnet_gmlp_shortconv_step"


def _split(t, sizes):
    cuts = [int(s) for s in np.cumsum(sizes)[:-1]]
    return jnp.split(t, cuts, axis=-1)


def rmsnorm(x, g):
    xf = x.astype(jnp.float32)
    y = xf * lax.rsqrt(jnp.mean(xf * xf, axis=-1, keepdims=True) + EPS)
    return (y * g.astype(jnp.float32)).astype(x.dtype)


def gla_chunked(q, k, v, log_a, s0):
    bsz, L, H, _ = q.shape
    dv = v.shape[-1]
    n = L // GLA_CHUNK
    rs = lambda t: t.reshape(bsz, n, GLA_CHUNK, H, t.shape[-1])
    q, k, v, log_a = rs(q), rs(k), rs(v), rs(log_a)
    b = jnp.cumsum(log_a, axis=2)
    b_last = b[:, :, -1:]
    qe = q * jnp.exp(b)
    ke = k * jnp.exp(-b)
    kd = k * jnp.exp(b_last - b)
    lower = jnp.tril(jnp.ones((GLA_CHUNK, GLA_CHUNK), dtype=bool))
    att = jnp.where(lower, jnp.einsum('bnihd,bnjhd->bnhij', qe, ke), 0.0)
    o_intra = jnp.einsum('bnhij,bnjhe->bnihe', att, v)
    u = jnp.einsum('bnjhd,bnjhe->bnhde', kd, v)
    decay = jnp.exp(b_last[:, :, 0])

    def step(s, inp):
        d, du = inp
        return d[..., None] * s + du, s

    s_final, s_prev = lax.scan(step, s0, (jnp.moveaxis(decay, 1, 0), jnp.moveaxis(u, 1, 0)))
    s_prev = jnp.moveaxis(s_prev, 0, 1)
    o_inter = jnp.einsum('bnihd,bnhde->bnihe', qe, s_prev)
    return (o_intra + o_inter).reshape(bsz, L, H, dv), s_final


def gla_bidir(q, k, v, la_f, la_b, s0_f, s0_b):
    o_f, s_f = gla_chunked(q, k, v, la_f, s0_f)
    flip = lambda t: jnp.flip(t, axis=1)
    o_b, s_b = gla_chunked(flip(q), flip(k), flip(v), flip(la_b), s0_b)
    return o_f + flip(o_b), s_f, s_b


def even_mixer(h, w_in, w_out, w2_f, b2_f, w2_b, b2_b, norm_g, s0_f, s0_b):
    bsz, L, _ = h.shape
    f32 = jnp.float32
    q, k, v, g, lr_f, lr_b, fin = _split(h @ w_in, EV_SIZES)
    heads = lambda t, d: t.reshape(bsz, L, GLA_HEADS, d).astype(f32)
    la_f = jax.nn.log_sigmoid((lr_f @ w2_f + b2_f).astype(f32)) / GLA_TAU
    la_b = jax.nn.log_sigmoid((lr_b @ w2_b + b2_b).astype(f32)) / GLA_TAU
    o, s_f, s_b = gla_bidir(heads(q, GLA_DK_HEAD) * (GLA_DK_HEAD ** -0.5),
                            heads(k, GLA_DK_HEAD), heads(v, GLA_DV_HEAD),
                            heads(la_f, GLA_DK_HEAD), heads(la_b, GLA_DK_HEAD),
                            s0_f.astype(f32), s0_b.astype(f32))
    o = o * lax.rsqrt(jnp.mean(o * o, axis=-1, keepdims=True) + EPS) * norm_g.astype(f32)
    o = o.reshape(bsz, L, GLA_DV).astype(h.dtype) * jax.nn.silu(g)
    fg = fin.reshape(bsz, L, FNET_GROUPS, FNET_GROUP_DIM).astype(f32)
    fo = jnp.fft.fftn(fg, axes=(1, 3), norm='ortho').real.reshape(bsz, L, FNET_WIDTH).astype(h.dtype)
    return jnp.concatenate([o, fo], axis=-1) @ w_out, s_f, s_b


def conv3(z, w):
    zp = jnp.pad(z, [(0, 0)] * (z.ndim - 2) + [(1, 1), (0, 0)])
    return zp[..., :-2, :] * w[0] + zp[..., 1:-1, :] * w[1] + zp[..., 2:, :] * w[2]


def odd_mixer(h, w_in, w_out, ws, bs, conv_w, rows):
    bsz, L, _ = h.shape
    uv, gb, gc, hin = _split(h @ w_in, OD_SIZES)
    u, v = jnp.split(jax.nn.gelu(uv), 2, axis=-1)
    n = L // GMLP_CHUNK
    vc = v.reshape(bsz, n, GMLP_CHUNK, GMLP_GROUPS, GMLP_GROUP_DIM)
    sp = jnp.einsum('gpq,bnqgc->bnpgc', ws, vc) + bs.T[:, :, None]
    out_c = u * sp.reshape(bsz, L, GMLP_WIDTH)
    z = gc * hin
    if rows is not None:
        z = z.reshape(bsz, rows, GRID_W, CONV_WIDTH)
    out_d = gb * conv3(z, conv_w).reshape(bsz, L, CONV_WIDTH)
    return jnp.concatenate([out_c, out_d], axis=-1) @ w_out


def setup_inputs(seed: int = 0) -> dict:
    key = jax.random.key(seed)
    ks = iter(jax.random.split(key, 40))
    nrm = lambda shape, s: jax.random.normal(next(ks), shape, jnp.float32) * s
    D = D_MODEL
    st_shape = (DEC_BATCH, N_EVEN, GLA_HEADS, GLA_DK_HEAD, GLA_DV_HEAD)
    return {
        "x_prompt": nrm((BATCH, SEQ, D), 1.0),
        "x_sample": nrm((DEC_BATCH, DEC_SEQ, D), 1.0),
        "state_gla_fwd": nrm(st_shape, 0.5),
        "state_gla_bwd": nrm(st_shape, 0.5),
        "c": nrm((DEC_BATCH, D), 1.0),
        "c_ctx": nrm((D,), 1.0),
        "ada_w": nrm((DEPTH, D, N_MOD * D), 0.5 * D ** -0.5),
        "ada_b": nrm((DEPTH, N_MOD * D), 0.02),
        "norm_mix_g": 1.0 + nrm((DEPTH, D), 0.02),
        "norm_ffn_g": 1.0 + nrm((DEPTH, D), 0.02),
        "ffn_w1": nrm((DEPTH, D, D_FF), D ** -0.5),
        "ffn_w2": nrm((DEPTH, D_FF, D), D_FF ** -0.5),
        "ev_w_in": nrm((N_EVEN, D, EV_IN), D ** -0.5),
        "ev_w_out": nrm((N_EVEN, D_MIX, D), D_MIX ** -0.5),
        "gla_w2_f": nrm((N_EVEN, GLA_LOWRANK, GLA_DK), GLA_LOWRANK ** -0.5),
        "gla_b2_f": nrm((N_EVEN, GLA_DK), 0.5),
        "gla_w2_b": nrm((N_EVEN, GLA_LOWRANK, GLA_DK), GLA_LOWRANK ** -0.5),
        "gla_b2_b": nrm((N_EVEN, GLA_DK), 0.5),
        "gla_norm_g": 1.0 + nrm((N_EVEN, GLA_DV_HEAD), 0.02),
        "od_w_in": nrm((N_ODD, D, OD_IN), D ** -0.5),
        "od_w_out": nrm((N_ODD, D_MIX, D), D_MIX ** -0.5),
        "gmlp_ws": nrm((N_ODD, GMLP_GROUPS, GMLP_CHUNK, GMLP_CHUNK), GMLP_CHUNK ** -0.5),
        "gmlp_b": 1.0 + nrm((N_ODD, GMLP_GROUPS, GMLP_CHUNK), 0.02),
        "conv_w": nrm((N_ODD, CONV_K, CONV_WIDTH), CONV_K ** -0.5),
        "final_norm_g": 1.0 + nrm((D,), 0.02),
    }


def reference(x_prompt, x_sample, state_gla_fwd, state_gla_bwd, c, c_ctx,
              ada_w, ada_b, norm_mix_g, norm_ffn_g, ffn_w1, ffn_w2,
              ev_w_in, ev_w_out, gla_w2_f, gla_b2_f, gla_w2_b, gla_b2_b, gla_norm_g,
              od_w_in, od_w_out, gmlp_ws, gmlp_b, conv_w, final_norm_g):

    def trunk(x, cond, s_in_f, s_in_b, rows):
        outs_f, outs_b = [], []
        for l in range(DEPTH):
            mod = (jax.nn.silu(cond) @ ada_w[l] + ada_b[l])[:, None, :]
            sh_m, sc_m, g_m, sh_f, sc_f, g_f = jnp.split(mod, N_MOD, axis=-1)
            h = rmsnorm(x, norm_mix_g[l]) * (1.0 + sc_m) + sh_m
            if l % 2 == 0:
                i = l // 2
                y, s_f, s_b = even_mixer(h, ev_w_in[i], ev_w_out[i], gla_w2_f[i], gla_b2_f[i],
                                         gla_w2_b[i], gla_b2_b[i], gla_norm_g[i],
                                         s_in_f[:, i], s_in_b[:, i])
                outs_f.append(s_f.astype(x.dtype))
                outs_b.append(s_b.astype(x.dtype))
            else:
                j = l // 2
                y = odd_mixer(h, od_w_in[j], od_w_out[j], gmlp_ws[j], gmlp_b[j], conv_w[j], rows)
            x = x + g_m * y
            h = rmsnorm(x, norm_ffn_g[l]) * (1.0 + sc_f) + sh_f
            x = x + g_f * (jnp.square(jax.nn.relu(h @ ffn_w1[l])) @ ffn_w2[l])
        return rmsnorm(x, final_norm_g), outs_f, outs_b

    b_ctx = x_prompt.shape[0]
    zero_state = jnp.zeros((b_ctx, N_EVEN, GLA_HEADS, GLA_DK_HEAD, GLA_DV_HEAD), jnp.float32)
    y_prompt, sf_list, sb_list = trunk(x_prompt, c_ctx[None, :], zero_state, zero_state, None)
    new_state_gla_fwd = jnp.stack(sf_list, axis=1)
    new_state_gla_bwd = jnp.stack(sb_list, axis=1)

    rows = x_sample.shape[1] // GRID_W
    y_sample, _, _ = trunk(x_sample, c, state_gla_fwd, state_gla_bwd, rows)

    return (y_prompt, y_sample, new_state_gla_fwd, new_state_gla_bwd)
```

```python
import functools

import jax
import jax.numpy as jnp
import numpy as np
from jax import lax
from jax.experimental import pallas as pl
from jax.experimental.pallas import tpu as pltpu

D = 1024
D_FF = 4 * D
EPS = 1e-6
N_MOD = 6
GRID_W = 64

H = 4
DK = 128
DV = 192
DVP = 256
GLA_LOWRANK = 16
GLA_TAU = 16.0
CH = 64
FW = 256
FG = 64
GW = 512
GCH = 128
CW = 512

TM = 256
LRP = 128
EV_COLS = 2 * H * DK + 2 * H * DVP + LRP + FW
MIX0 = H * DVP + FW
OD_COLS = 2 * GW + 3 * CW

VMEM_LIMIT = 56 * 1024 * 1024

BF = jnp.bfloat16
F32 = jnp.float32


def _dot(a, b):
    return jnp.dot(a, b, preferred_element_type=F32)


def _dot_nt(a, b):
    return lax.dot_general(a, b, (((1,), (1,)), ((), ())), preferred_element_type=F32)


def _dot_tn(a, b):
    return lax.dot_general(a, b, (((0,), (0,)), ((), ())), preferred_element_type=F32)


def _sigmoid(x):
    return 1.0 / (1.0 + jnp.exp(-x))


def _silu(x):
    return x * _sigmoid(x)


def _log_sigmoid(x):
    return jnp.minimum(x, 0.0) - jnp.log(1.0 + jnp.exp(-jnp.abs(x)))


def _gelu_tanh(x):
    c = np.float32(np.sqrt(2.0 / np.pi))
    return 0.5 * x * (1.0 + jnp.tanh(c * (x + 0.044715 * (x * x * x))))


def _rms_mod(x, g, sc, sh):
    ms = jnp.mean(x * x, axis=-1, keepdims=True)
    return (x * lax.rsqrt(ms + EPS) * g) * (1.0 + sc) + sh


def _mod_kernel(c_ref, w_ref, b_ref, o_ref):
    a = _silu(c_ref[...]).astype(BF)
    o_ref[0] = _dot(a, w_ref[0].astype(BF)) + b_ref[0]


def _modulation(cond, ada_w, ada_b):
    depth = ada_w.shape[0]
    r = cond.shape[0]
    tn = 1024
    return pl.pallas_call(
        _mod_kernel,
        grid=(depth, N_MOD * D // tn),
        in_specs=[
            pl.BlockSpec((r, D), lambda l, j: (0, 0)),
            pl.BlockSpec((1, D, tn), lambda l, j: (l, 0, j)),
            pl.BlockSpec((1, 1, tn), lambda l, j: (l, 0, j)),
        ],
        out_specs=pl.BlockSpec((1, r, tn), lambda l, j: (l, 0, j)),
        out_shape=jax.ShapeDtypeStruct((depth, r, N_MOD * D), F32),
        compiler_params=pltpu.CompilerParams(
            dimension_semantics=("arbitrary", "arbitrary"), vmem_limit_bytes=VMEM_LIMIT),
        name="modulation",
    )(cond, ada_w, ada_b.reshape(depth, 1, N_MOD * D))


def _chunk_scan(x, pos, reverse):
    n = x.shape[0]
    s = 1
    while s < CH:
        if reverse:
            y = pltpu.roll(x, n - s, axis=0)
            x = x + jnp.where(pos < CH - s, y, 0.0)
        else:
            y = pltpu.roll(x, s, axis=0)
            x = x + jnp.where(pos >= s, y, 0.0)
        s *= 2
    return x


def _ev_in_kernel(x_ref, mod_ref, ng_ref, w_ref, w2_ref, b2_ref, dft_ref,
                  qef_ref, kef_ref, kdf_ref, qeb_ref, keb_ref, kdb_ref,
                  decf_ref, decb_ref, v_ref, g_ref, xcs_ref):
    x = x_ref[0]
    mod = mod_ref[0]
    h = _rms_mod(x, ng_ref[...], mod[1:2], mod[0:1]).astype(BF)
    proj = _dot(h, w_ref[...])
    c0 = H * DK
    q = proj[:, 0:c0] * np.float32(DK ** -0.5)
    k = proj[:, c0:2 * c0]
    o = 2 * c0
    v_ref[0] = proj[:, o:o + H * DVP].astype(BF)
    o += H * DVP
    g_ref[0] = proj[:, o:o + H * DVP].astype(BF)
    o += H * DVP
    lr = proj[:, o:o + LRP].astype(BF)
    o += LRP
    fin = proj[:, o:o + FW].astype(BF)
    xcs_ref[0] = _dot(fin, dft_ref[...]).astype(BF)

    pre = _dot(lr, w2_ref[...]) + b2_ref[...]
    la = _log_sigmoid(pre) * np.float32(1.0 / GLA_TAU)
    pos = lax.broadcasted_iota(jnp.int32, (TM, c0), 0) & (CH - 1)
    nch = TM // CH

    bf = _chunk_scan(la[:, :c0], pos, False)
    bf3 = bf.reshape(nch, CH, c0)
    lastf = bf3[:, CH - 1:CH, :]
    qef_ref[0] = (q * jnp.exp(bf)).astype(BF)
    kef_ref[0] = (k * jnp.exp(-bf)).astype(BF)
    kdf_ref[0] = (k * jnp.exp(jnp.broadcast_to(lastf, (nch, CH, c0)) - bf3).reshape(TM, c0)).astype(BF)
    decf_ref[0] = jnp.exp(lastf)

    bb = _chunk_scan(la[:, c0:], pos, True)
    bb3 = bb.reshape(nch, CH, c0)
    firstb = bb3[:, 0:1, :]
    qeb_ref[0] = (q * jnp.exp(bb)).astype(BF)
    keb_ref[0] = (k * jnp.exp(-bb)).astype(BF)
    kdb_ref[0] = (k * jnp.exp(jnp.broadcast_to(firstb, (nch, CH, c0)) - bb3).reshape(TM, c0)).astype(BF)
    decb_ref[0] = jnp.exp(firstb)


def _ev_in(x, mods, rows, ng, w_in, w2cat, b2cat, dftc):
    b, l, _ = x.shape
    nt = l // TM
    nch = TM // CH
    c0 = H * DK
    tok = lambda w: pl.BlockSpec((1, TM, w), lambda i, j: (i, j, 0))
    const = lambda shape: pl.BlockSpec(shape, lambda i, j: tuple(0 for _ in shape))
    dec_spec = pl.BlockSpec((1, nch, 1, c0), lambda i, j: (i * nt + j, 0, 0, 0))
    tok_shape = lambda w: jax.ShapeDtypeStruct((b, l, w), BF)
    dec_shape = jax.ShapeDtypeStruct((b * nt, nch, 1, c0), F32)
    return pl.pallas_call(
        _ev_in_kernel,
        grid=(b, nt),
        in_specs=[
            tok(D),
            pl.BlockSpec((1, N_MOD, D), lambda i, j: (rows(i), 0, 0)),
            const((1, D)), const((D, EV_COLS)), const((LRP, 2 * c0)), const((1, 2 * c0)),
            const((FW, 2 * FW)),
        ],
        out_specs=[tok(c0)] * 6 + [dec_spec, dec_spec, tok(H * DVP), tok(H * DVP), tok(2 * FW)],
        out_shape=[tok_shape(c0)] * 6 + [dec_shape, dec_shape,
                                         tok_shape(H * DVP), tok_shape(H * DVP), tok_shape(2 * FW)],
        compiler_params=pltpu.CompilerParams(
            dimension_semantics=("arbitrary", "arbitrary"), vmem_limit_bytes=VMEM_LIMIT),
        name="ev_in",
    )(x, mods, ng, w_in, w2cat, b2cat, dftc)


def _gla_kernel(l, qef_ref, kef_ref, kdf_ref, qeb_ref, keb_ref, kdb_ref, decf_ref, decb_ref,
                v_ref, g_ref, xcs_ref, s0f_ref, s0b_ref, gn_ref, cl_ref, sl_ref,
                mix_ref, sf_ref, sb_ref, st_ref, o_ref):
    nchunks = l // CH
    nch_t = TM // CH
    st_ref[0] = s0f_ref[0]
    st_ref[1] = s0b_ref[0]
    o_ref[...] = jnp.zeros_like(o_ref)

    row = lax.broadcasted_iota(jnp.int32, (CH, CH), 0)
    col = lax.broadcasted_iota(jnp.int32, (CH, CH), 1)
    masks = (row >= col, row <= col)
    dirs = ((qef_ref, kef_ref, kdf_ref, decf_ref), (qeb_ref, keb_ref, kdb_ref, decb_ref))

    def body(i, carry):
        for d in range(2):
            qe_ref, ke_ref, kd_ref, dec_ref = dirs[d]
            n = i if d == 0 else nchunks - 1 - i
            r0 = pl.multiple_of(n * CH, CH)
            dec = dec_ref[n // nch_t, n % nch_t]
            for hh in range(H):
                ks = slice(hh * DK, (hh + 1) * DK)
                vs = slice(hh * DVP, (hh + 1) * DVP)
                qe = qe_ref[0, pl.ds(r0, CH), ks]
                ke = ke_ref[0, pl.ds(r0, CH), ks]
                kd = kd_ref[0, pl.ds(r0, CH), ks]
                vv = v_ref[0, pl.ds(r0, CH), vs]
                st = st_ref[d, hh]
                att = jnp.where(masks[d], _dot_nt(qe, ke), 0.0).astype(BF)
                o = _dot(att, vv) + _dot_nt(qe, st.astype(BF))
                o_ref[pl.ds(r0, CH), vs] += o
                st_ref[d, hh] = dec[:, ks] * st + _dot_tn(vv, kd)
        return carry

    lax.fori_loop(0, nchunks, body, 0)

    sf_ref[0] = st_ref[0]
    sb_ref[0] = st_ref[1]

    fo = _dot(cl_ref[...], xcs_ref[0, :, 0:FW]) + _dot(sl_ref[...], xcs_ref[0, :, FW:2 * FW])
    mix_ref[0, :, H * DVP:H * DVP + FW] = fo.astype(BF)

    gn = gn_ref[...]
    rb = 128

    def norm_body(t, carry):
        r0 = pl.multiple_of(t * rb, rb)
        for hh in range(H):
            vs = slice(hh * DVP, (hh + 1) * DVP)
            o = o_ref[pl.ds(r0, rb), vs]
            ms = jnp.sum(o * o, axis=-1, keepdims=True) * np.float32(1.0 / DV)
            on = o * lax.rsqrt(ms + EPS) * gn
            gate = _silu(g_ref[0, pl.ds(r0, rb), vs].astype(F32))
            mix_ref[0, pl.ds(r0, rb), vs] = (on * gate).astype(BF)
        return carry

    lax.fori_loop(0, l // rb, norm_body, 0)


def _gla(l, qef, kef, kdf, qeb, keb, kdb, decf, decb, v, g, xcs, s0f, s0b, has_state, gn, cl, sl):
    b = qef.shape[0]
    nt = l // TM
    nch_t = TM // CH
    c0 = H * DK
    seq = lambda w: pl.BlockSpec((1, l, w), lambda i: (i, 0, 0))
    dec_spec = pl.BlockSpec((nt, nch_t, 1, c0), lambda i: (i, 0, 0, 0))
    st_idx = (lambda i: (i, 0, 0, 0)) if has_state else (lambda i: (0, 0, 0, 0))
    st_in = pl.BlockSpec((1, H, DVP, DK), st_idx)
    st_out = pl.BlockSpec((1, H, DVP, DK), lambda i: (i, 0, 0, 0))
    const = lambda shape: pl.BlockSpec(shape, lambda i: tuple(0 for _ in shape))
    st_shape = jax.ShapeDtypeStruct((b, H, DVP, DK), F32)
    return pl.pallas_call(
        functools.partial(_gla_kernel, l),
        grid=(b,),
        in_specs=[seq(c0)] * 6 + [dec_spec, dec_spec, seq(H * DVP), seq(H * DVP), seq(2 * FW),
                                  st_in, st_in, const((1, DVP)), const((l, l)), const((l, l))],
        out_specs=[seq(MIX0), st_out, st_out],
        out_shape=[jax.ShapeDtypeStruct((b, l, MIX0), BF), st_shape, st_shape],
        scratch_shapes=[pltpu.VMEM((2, H, DVP, DK), F32), pltpu.VMEM((l, H * DVP), F32)],
        compiler_params=pltpu.CompilerParams(
            dimension_semantics=("arbitrary",), vmem_limit_bytes=VMEM_LIMIT),
        name="gla_fnet",
    )(qef, kef, kdf, qeb, keb, kdb, decf, decb, v, g, xcs, s0f, s0b, gn, cl, sl)


def _ffn(x1, mod, ng, w1_ref, w2_ref):
    h = _rms_mod(x1, ng, mod[4:5], mod[3:4]).astype(BF)
    hid = jnp.maximum(_dot(h, w1_ref[...]), 0.0)
    hid = (hid * hid).astype(BF)
    return x1 + mod[5:6] * _dot(hid, w2_ref[...])


def _ev_out_kernel(x_ref, mix_ref, mod_ref, ng_ref, wo_ref, w1_ref, w2_ref, o_ref):
    mod = mod_ref[0]
    x1 = x_ref[0] + mod[2:3] * _dot(mix_ref[0], wo_ref[...])
    o_ref[0] = _ffn(x1, mod, ng_ref[...], w1_ref, w2_ref)


def _ev_out(x, mix, mods, rows, ng, wo, w1, w2):
    b, l, _ = x.shape
    nt = l // TM
    tok = lambda w: pl.BlockSpec((1, TM, w), lambda i, j: (i, j, 0))
    const = lambda shape: pl.BlockSpec(shape, lambda i, j: tuple(0 for _ in shape))
    return pl.pallas_call(
        _ev_out_kernel,
        grid=(b, nt),
        in_specs=[tok(D), tok(MIX0), pl.BlockSpec((1, N_MOD, D), lambda i, j: (rows(i), 0, 0)),
                  const((1, D)), const((MIX0, D)), const((D, D_FF)), const((D_FF, D))],
        out_specs=tok(D),
        out_shape=jax.ShapeDtypeStruct((b, l, D), F32),
        compiler_params=pltpu.CompilerParams(
            dimension_semantics=("arbitrary", "arbitrary"), vmem_limit_bytes=VMEM_LIMIT),
        name="ev_out_ffn",
    )(x, mix, mods, ng, wo, w1, w2)


def _odd_kernel(rows_len, x_ref, mod_ref, ngm_ref, ngf_ref, gfin_ref, wi_ref, ws_ref, gb_ref,
                cw_ref, wo_ref, w1_ref, w2_ref, o_ref):
    x = x_ref[0]
    mod = mod_ref[0]
    h = _rms_mod(x, ngm_ref[...], mod[1:2], mod[0:1]).astype(BF)
    proj = _dot(h, wi_ref[...])
    u = _gelu_tanh(proj[:, 0:GW])
    v = _gelu_tanh(proj[:, GW:2 * GW]).astype(BF)
    gate_b = proj[:, 2 * GW:2 * GW + CW]
    z = proj[:, 2 * GW + CW:2 * GW + 2 * CW] * proj[:, 2 * GW + 2 * CW:2 * GW + 3 * CW]

    sp_rows = []
    for c in range(TM // GCH):
        cols = []
        for gi in range(GW // GCH):
            vg = v[c * GCH:(c + 1) * GCH, gi * GCH:(gi + 1) * GCH]
            cols.append(_dot(ws_ref[gi], vg))
        sp_rows.append(jnp.concatenate(cols, axis=1) + gb_ref[...])
    out_c = u * jnp.concatenate(sp_rows, axis=0)

    pos = lax.broadcasted_iota(jnp.int32, (TM, CW), 0) & (rows_len - 1)
    zl = jnp.where(pos >= 1, pltpu.roll(z, 1, axis=0), 0.0)
    zr = jnp.where(pos < rows_len - 1, pltpu.roll(z, TM - 1, axis=0), 0.0)
    cw = cw_ref[...]
    out_d = gate_b * (zl * cw[0:1] + z * cw[1:2] + zr * cw[2:3])

    mix = jnp.concatenate([out_c, out_d], axis=1).astype(BF)
    x1 = x + mod[2:3] * _dot(mix, wo_ref[...])
    x2 = _ffn(x1, mod, ngf_ref[...], w1_ref, w2_ref)
    ms = jnp.mean(x2 * x2, axis=-1, keepdims=True)
    o_ref[0] = x2 * lax.rsqrt(ms + EPS) * gfin_ref[...]


def _odd(x, mods, rows, rows_len, ngm, ngf, gfin, wi, ws, gbias, cw, wo, w1, w2):
    b, l, _ = x.shape
    nt = l // TM
    assert TM % rows_len == 0 and rows_len & (rows_len - 1) == 0 and l % TM == 0
    tok = lambda w: pl.BlockSpec((1, TM, w), lambda i, j: (i, j, 0))
    const = lambda shape: pl.BlockSpec(shape, lambda i, j: tuple(0 for _ in shape))
    return pl.pallas_call(
        functools.partial(_odd_kernel, rows_len),
        grid=(b, nt),
        in_specs=[tok(D), pl.BlockSpec((1, N_MOD, D), lambda i, j: (rows(i), 0, 0)),
                  const((1, D)), const((1, D)), const((1, D)), const((D, OD_COLS)),
                  const((GW // GCH, GCH, GCH)), const((GCH, GW)), const((3, CW)),
                  const((D, D)), const((D, D_FF)), const((D_FF, D))],
        out_specs=tok(D),
        out_shape=jax.ShapeDtypeStruct((b, l, D), F32),
        compiler_params=pltpu.CompilerParams(
            dimension_semantics=("arbitrary", "arbitrary"), vmem_limit_bytes=VMEM_LIMIT),
        name="odd_ffn_final",
    )(x, mods, ngm, ngf, gfin, wi, ws, gbias, cw, wo, w1, w2)


def _pad_heads(w, axis):
    shape = list(w.shape)
    shape[axis:axis + 1] = [H, DV]
    w = w.reshape(shape)
    pad = [(0, 0)] * w.ndim
    pad[axis + 1] = (0, DVP - DV)
    w = jnp.pad(w, pad)
    shape[axis:axis + 2] = [H * DVP]
    return w.reshape(shape)


def _dft_consts(l):
    n = np.arange(l)
    ang = 2.0 * np.pi * ((n[:, None] * n[None, :]) % l) / l
    cl = np.cos(ang) / np.sqrt(l)
    sl = -np.sin(ang) / np.sqrt(l)
    m = np.arange(FG)
    angc = 2.0 * np.pi * ((m[:, None] * m[None, :]) % FG) / FG
    eye = np.eye(FW // FG)
    cc = np.kron(eye, np.cos(angc)) / np.sqrt(FG)
    sc = np.kron(eye, np.sin(angc)) / np.sqrt(FG)
    to_bf = lambda a: jnp.asarray(a, F32).astype(BF)
    return to_bf(cl), to_bf(sl), to_bf(np.concatenate([cc, sc], axis=1))


def _state_in(s):
    s = jnp.swapaxes(s.astype(F32), -1, -2)
    return jnp.pad(s, ((0, 0), (0, 0), (0, DVP - DV), (0, 0)))


def _state_out(s):
    return jnp.swapaxes(s[:, :, :DV, :], -1, -2)[:, None]


def kernel(x_prompt, x_sample, state_gla_fwd, state_gla_bwd, c, c_ctx, ada_w, ada_b, norm_mix_g,
           norm_ffn_g, ffn_w1, ffn_w2, ev_w_in, ev_w_out, gla_w2_f, gla_b2_f, gla_w2_b, gla_b2_b,
           gla_norm_g, od_w_in, od_w_out, gmlp_ws, gmlp_b, conv_w, final_norm_g):
    c0 = H * DK
    n_dec = c.shape[0]

    wi0 = ev_w_in[0]
    o = 2 * c0
    w_q_k = wi0[:, :o]
    w_v = _pad_heads(wi0[:, o:o + H * DV], 1)
    w_g = _pad_heads(wi0[:, o + H * DV:o + 2 * H * DV], 1)
    o += 2 * H * DV
    w_lr = jnp.pad(wi0[:, o:o + 2 * GLA_LOWRANK], ((0, 0), (0, LRP - 2 * GLA_LOWRANK)))
    w_fin = wi0[:, o + 2 * GLA_LOWRANK:]
    w_in0 = jnp.concatenate([w_q_k, w_v, w_g, w_lr, w_fin], axis=1).astype(BF)
    w2cat = jnp.zeros((LRP, 2 * c0), F32)
    w2cat = w2cat.at[:GLA_LOWRANK, :c0].set(gla_w2_f[0])
    w2cat = w2cat.at[GLA_LOWRANK:2 * GLA_LOWRANK, c0:].set(gla_w2_b[0]).astype(BF)
    b2cat = jnp.concatenate([gla_b2_f[0], gla_b2_b[0]])[None, :]
    wo0 = ev_w_out[0]
    w_out0 = jnp.concatenate([_pad_heads(wo0[:H * DV], 0), wo0[H * DV:]], axis=0).astype(BF)
    gn = jnp.pad(gla_norm_g[0], (0, DVP - DV))[None, :]
    w1 = ffn_w1.astype(BF)
    w2 = ffn_w2.astype(BF)
    wi1 = od_w_in[0].astype(BF)
    wo1 = od_w_out[0].astype(BF)
    ws = gmlp_ws[0].astype(BF)
    gbias = jnp.repeat(gmlp_b[0].T, GCH, axis=1)
    cw = conv_w[0]
    ngm = norm_mix_g[:, None, :]
    ngf = norm_ffn_g[:, None, :]
    gfin = final_norm_g[None, :]

    cond = jnp.concatenate([c_ctx[None, :], c], axis=0)
    cond = jnp.pad(cond, ((0, 16 - cond.shape[0]), (0, 0)))
    mods = _modulation(cond, ada_w, ada_b).reshape(ada_w.shape[0], 16, N_MOD, D)

    def trunk(x, rows, rows_len, s0f, s0b, has_state):
        l = x.shape[1]
        cl, sl, dftc = _dft_consts(l)
        (qef, kef, kdf, qeb, keb, kdb, decf, decb, v, g, xcs) = _ev_in(
            x, mods[0], rows, ngm[0], w_in0, w2cat, b2cat, dftc)
        mix, sf, sb = _gla(l, qef, kef, kdf, qeb, keb, kdb, decf, decb, v, g, xcs,
                           s0f, s0b, has_state, gn, cl, sl)
        x = _ev_out(x, mix, mods[0], rows, ngf[0], w_out0, w1[0], w2[0])
        y = _odd(x, mods[1], rows, rows_len, ngm[1], ngf[1], gfin, wi1, ws, gbias, cw, wo1,
                 w1[1], w2[1])
        return y, sf, sb

    zero_state = jnp.zeros((1, H, DVP, DK), F32)
    y_prompt, sf, sb = trunk(x_prompt, lambda i: 0, x_prompt.shape[1], zero_state, zero_state, False)
    y_sample, _, _ = trunk(x_sample, lambda i: i + 1, GRID_W,
                           _state_in(state_gla_fwd[:, 0]), _state_in(state_gla_bwd[:, 0]), True)
    del n_dec
    return (y_prompt, y_sample, _state_out(sf), _state_out(sb))
```

```python
import functools

import jax
import jax.numpy as jnp
import numpy as np
from jax import lax
from jax.experimental import pallas as pl
from jax.experimental.pallas import tpu as pltpu

D = 1024
D_FF = 4 * D
EPS = 1e-6
N_MOD = 6
GRID_W = 64

H = 4
DK = 128
DV = 192
DVP = 256
GLA_LOWRANK = 16
GLA_TAU = 16.0
CH = 64
FW = 256
FG = 64
GW = 512
GCH = 128
CW = 512

TM = 256
LRP = 128
MIX0 = H * DVP + FW
OD_COLS = 2 * GW + 3 * CW
GLA_UNROLL = 8

VMEM_LIMIT = 56 * 1024 * 1024

BF = jnp.bfloat16
F32 = jnp.float32


def _dot(a, b):
    return jnp.dot(a, b, preferred_element_type=F32)


def _dot_nt(a, b):
    return lax.dot_general(a, b, (((1,), (1,)), ((), ())), preferred_element_type=F32)


def _dot_tn(a, b):
    return lax.dot_general(a, b, (((0,), (0,)), ((), ())), preferred_element_type=F32)


def _sigmoid(x):
    return 1.0 / (1.0 + jnp.exp(-x))


def _silu(x):
    return x * _sigmoid(x)


def _log_sigmoid(x):
    return jnp.minimum(x, 0.0) - jnp.log(1.0 + jnp.exp(-jnp.abs(x)))


def _gelu_tanh(x):
    c = np.float32(np.sqrt(2.0 / np.pi))
    return 0.5 * x * (1.0 + jnp.tanh(c * (x + 0.044715 * (x * x * x))))


def _rms_mod(x, g, sc, sh):
    ms = jnp.mean(x * x, axis=-1, keepdims=True)
    return (x * lax.rsqrt(ms + EPS)) * (g * (1.0 + sc)) + sh


def _mod_kernel(c_ref, w_ref, b_ref, o_ref):
    a = _silu(c_ref[...]).astype(BF)
    o_ref[0] = _dot(a, w_ref[0].astype(BF)) + b_ref[0]


def _modulation(cond, ada_w, ada_b):
    depth = ada_w.shape[0]
    r = cond.shape[0]
    tn = 1024
    return pl.pallas_call(
        _mod_kernel,
        grid=(depth, N_MOD * D // tn),
        in_specs=[
            pl.BlockSpec((r, D), lambda l, j: (0, 0)),
            pl.BlockSpec((1, D, tn), lambda l, j: (l, 0, j)),
            pl.BlockSpec((1, 1, tn), lambda l, j: (l, 0, j)),
        ],
        out_specs=pl.BlockSpec((1, r, tn), lambda l, j: (l, 0, j)),
        out_shape=jax.ShapeDtypeStruct((depth, r, N_MOD * D), F32),
        compiler_params=pltpu.CompilerParams(
            dimension_semantics=("arbitrary", "arbitrary"), vmem_limit_bytes=VMEM_LIMIT),
        name="modulation",
    )(cond, ada_w, ada_b.reshape(depth, 1, N_MOD * D))


def _ev_in_kernel(x_ref, mod_ref, ng_ref, wlr_ref, wqk_ref, wvg_ref, wfin_ref, w2_ref, b2_ref,
                  dft_ref, q_ref, k_ref, la_ref, v_ref, g_ref, xcs_ref):
    x = x_ref[0]
    mod = mod_ref[0]
    h = _rms_mod(x, ng_ref[...], mod[1:2], mod[0:1]).astype(BF)
    c0 = H * DK

    lr = _dot(h, wlr_ref[...]).astype(BF)
    pre = _dot(lr, w2_ref[...]) + b2_ref[...]
    la_ref[0] = _log_sigmoid(pre) * np.float32(1.0 / GLA_TAU)

    qk = _dot(h, wqk_ref[...])
    q_ref[0] = (qk[:, 0:c0] * np.float32(DK ** -0.5)).astype(BF)
    k_ref[0] = qk[:, c0:2 * c0].astype(BF)
    vg = _dot(h, wvg_ref[...])
    v_ref[0] = vg[:, 0:H * DVP].astype(BF)
    g_ref[0] = vg[:, H * DVP:2 * H * DVP].astype(BF)
    fin = _dot(h, wfin_ref[...]).astype(BF)
    xcs_ref[0] = _dot(fin, dft_ref[...]).astype(BF)


def _ev_in(x, mods, rows, ng, wlr, wqk, wvg, wfin, w2cat, b2cat, dftc):
    b, l, _ = x.shape
    nt = l // TM
    c0 = H * DK
    tok = lambda w: pl.BlockSpec((1, TM, w), lambda i, j: (i, j, 0))
    const = lambda shape: pl.BlockSpec(shape, lambda i, j: tuple(0 for _ in shape))
    tok_shape = lambda w, dt: jax.ShapeDtypeStruct((b, l, w), dt)
    return pl.pallas_call(
        _ev_in_kernel,
        grid=(b, nt),
        in_specs=[
            tok(D),
            pl.BlockSpec((1, N_MOD, D), lambda i, j: (rows(i), 0, 0)),
            const((1, D)), const((D, LRP)), const((D, 2 * c0)), const((D, 2 * H * DVP)),
            const((D, FW)), const((LRP, 2 * c0)), const((1, 2 * c0)), const((FW, 2 * FW)),
        ],
        out_specs=[tok(c0), tok(c0), tok(2 * c0), tok(H * DVP), tok(H * DVP), tok(2 * FW)],
        out_shape=[tok_shape(c0, BF), tok_shape(c0, BF), tok_shape(2 * c0, F32),
                   tok_shape(H * DVP, BF), tok_shape(H * DVP, BF), tok_shape(2 * FW, BF)],
        compiler_params=pltpu.CompilerParams(
            dimension_semantics=("arbitrary", "arbitrary"), vmem_limit_bytes=VMEM_LIMIT),
        name="ev_in",
    )(x, mods, ng, wlr, wqk, wvg, wfin, w2cat, b2cat, dftc)


def _chunk_scan(x, pos, reverse):
    s = 1
    while s < CH:
        if reverse:
            y = pltpu.roll(x, CH - s, axis=0)
            x = x + jnp.where(pos < CH - s, y, 0.0)
        else:
            y = pltpu.roll(x, s, axis=0)
            x = x + jnp.where(pos >= s, y, 0.0)
        s *= 2
    return x


def _gla_kernel(l, q_ref, k_ref, la_ref, v_ref, g_ref, xcs_ref, s0f_ref, s0b_ref, gn_ref,
                cl_ref, sl_ref, mix_ref, sf_ref, sb_ref,
                bc_ref, ke_ref, dec_ref, u_ref, sp_ref, st_ref):
    nchunks = l // CH
    unroll = min(GLA_UNROLL, nchunks)
    pos = lax.broadcasted_iota(jnp.int32, (CH, DK), 0)
    row = lax.broadcasted_iota(jnp.int32, (CH, CH), 0)
    col = lax.broadcasted_iota(jnp.int32, (CH, CH), 1)
    gn = gn_ref[...]

    for hh in range(H):
        ks = slice(hh * DK, (hh + 1) * DK)
        vs = slice(hh * DVP, (hh + 1) * DVP)

        def inc_body(n, carry):
            r0 = pl.multiple_of(n * CH, CH)
            la = la_ref[0, pl.ds(r0, CH), vs]
            bf = _chunk_scan(la[:, 0:DK], pos, False)
            bb = _chunk_scan(la[:, DK:2 * DK], pos, True)
            bc_ref[pl.ds(r0, CH), 0:DK] = bf
            bc_ref[pl.ds(r0, CH), DK:2 * DK] = bb
            dec_ref[n, :, 0:DK] = jnp.exp(bf[CH - 1:CH, :])
            dec_ref[n, :, DK:2 * DK] = jnp.exp(bb[0:1, :])
            kk = k_ref[0, pl.ds(r0, CH), ks].astype(F32)
            ke = jnp.concatenate([kk * jnp.exp(-bf), kk * jnp.exp(-bb)], axis=1).astype(BF)
            ke_ref[pl.ds(r0, CH), :] = ke
            u_ref[n] = _dot_tn(v_ref[0, pl.ds(r0, CH), vs], ke)
            return carry

        lax.fori_loop(0, nchunks, inc_body, 0, unroll=unroll)

        st_ref[:, 0:DK] = s0f_ref[0, hh]
        st_ref[:, DK:2 * DK] = s0b_ref[0, hh]

        def scan_body(i, carry):
            nf = i
            nb = nchunks - 1 - i
            sf = st_ref[:, 0:DK]
            sb = st_ref[:, DK:2 * DK]
            sp_ref[nf, :, 0:DK] = sf.astype(BF)
            sp_ref[nb, :, DK:2 * DK] = sb.astype(BF)
            st_ref[:, 0:DK] = dec_ref[nf, :, 0:DK] * (sf + u_ref[nf, :, 0:DK])
            st_ref[:, DK:2 * DK] = dec_ref[nb, :, DK:2 * DK] * (sb + u_ref[nb, :, DK:2 * DK])
            return carry

        lax.fori_loop(0, nchunks, scan_body, 0)
        sf_ref[0, hh] = st_ref[:, 0:DK]
        sb_ref[0, hh] = st_ref[:, DK:2 * DK]

        def out_body(n, carry):
            r0 = pl.multiple_of(n * CH, CH)
            bc = bc_ref[pl.ds(r0, CH), :]
            qq = q_ref[0, pl.ds(r0, CH), ks].astype(F32)
            qe = jnp.concatenate([qq * jnp.exp(bc[:, 0:DK]), qq * jnp.exp(bc[:, DK:2 * DK])],
                                 axis=1).astype(BF)
            ke = ke_ref[pl.ds(r0, CH), :]
            att = (jnp.where(row >= col, _dot_nt(qe[:, 0:DK], ke[:, 0:DK]), 0.0)
                   + jnp.where(row <= col, _dot_nt(qe[:, DK:], ke[:, DK:]), 0.0)).astype(BF)
            o = _dot(att, v_ref[0, pl.ds(r0, CH), vs]) + _dot_nt(qe, sp_ref[n])
            ms = jnp.sum(o * o, axis=-1, keepdims=True) * np.float32(1.0 / DV)
            on = o * lax.rsqrt(ms + EPS) * gn
            gate = _silu(g_ref[0, pl.ds(r0, CH), vs].astype(F32))
            mix_ref[0, pl.ds(r0, CH), vs] = (on * gate).astype(BF)
            return carry

        lax.fori_loop(0, nchunks, out_body, 0, unroll=unroll)

    fo = _dot(cl_ref[...], xcs_ref[0, :, 0:FW]) + _dot(sl_ref[...], xcs_ref[0, :, FW:2 * FW])
    mix_ref[0, :, H * DVP:H * DVP + FW] = fo.astype(BF)


def _gla(l, q, k, la, v, g, xcs, s0f, s0b, has_state, gn, cl, sl):
    b = q.shape[0]
    nchunks = l // CH
    c0 = H * DK
    seq = lambda w: pl.BlockSpec((1, l, w), lambda i: (i, 0, 0))
    st_idx = (lambda i: (i, 0, 0, 0)) if has_state else (lambda i: (0, 0, 0, 0))
    st_in = pl.BlockSpec((1, H, DVP, DK), st_idx)
    st_out = pl.BlockSpec((1, H, DVP, DK), lambda i: (i, 0, 0, 0))
    const = lambda shape: pl.BlockSpec(shape, lambda i: tuple(0 for _ in shape))
    st_shape = jax.ShapeDtypeStruct((b, H, DVP, DK), F32)
    return pl.pallas_call(
        functools.partial(_gla_kernel, l),
        grid=(b,),
        in_specs=[seq(c0), seq(c0), seq(2 * c0), seq(H * DVP), seq(H * DVP), seq(2 * FW),
                  st_in, st_in, const((1, DVP)), const((l, l)), const((l, l))],
        out_specs=[seq(MIX0), st_out, st_out],
        out_shape=[jax.ShapeDtypeStruct((b, l, MIX0), BF), st_shape, st_shape],
        scratch_shapes=[pltpu.VMEM((l, 2 * DK), F32),
                        pltpu.VMEM((l, 2 * DK), BF),
                        pltpu.VMEM((nchunks, 1, 2 * DK), F32),
                        pltpu.VMEM((nchunks, DVP, 2 * DK), F32),
                        pltpu.VMEM((nchunks, DVP, 2 * DK), BF),
                        pltpu.VMEM((DVP, 2 * DK), F32)],
        compiler_params=pltpu.CompilerParams(
            dimension_semantics=("arbitrary",), vmem_limit_bytes=VMEM_LIMIT),
        name="gla_fnet",
    )(q, k, la, v, g, xcs, s0f, s0b, gn, cl, sl)


def _ffn(x1, mod, ng, w1_ref, w2_ref):
    h = _rms_mod(x1, ng, mod[4:5], mod[3:4]).astype(BF)
    hid = jnp.maximum(_dot(h, w1_ref[...]), 0.0)
    hid = (hid * hid).astype(BF)
    return x1 + mod[5:6] * _dot(hid, w2_ref[...])


def _ev_out_kernel(x_ref, mix_ref, mod_ref, ng_ref, wo_ref, w1_ref, w2_ref, o_ref):
    mod = mod_ref[0]
    x1 = x_ref[0] + mod[2:3] * _dot(mix_ref[0], wo_ref[...])
    o_ref[0] = _ffn(x1, mod, ng_ref[...], w1_ref, w2_ref)


def _ev_out(x, mix, mods, rows, ng, wo, w1, w2):
    b, l, _ = x.shape
    nt = l // TM
    tok = lambda w: pl.BlockSpec((1, TM, w), lambda i, j: (i, j, 0))
    const = lambda shape: pl.BlockSpec(shape, lambda i, j: tuple(0 for _ in shape))
    return pl.pallas_call(
        _ev_out_kernel,
        grid=(b, nt),
        in_specs=[tok(D), tok(MIX0), pl.BlockSpec((1, N_MOD, D), lambda i, j: (rows(i), 0, 0)),
                  const((1, D)), const((MIX0, D)), const((D, D_FF)), const((D_FF, D))],
        out_specs=tok(D),
        out_shape=jax.ShapeDtypeStruct((b, l, D), F32),
        compiler_params=pltpu.CompilerParams(
            dimension_semantics=("arbitrary", "arbitrary"), vmem_limit_bytes=VMEM_LIMIT),
        name="ev_out_ffn",
    )(x, mix, mods, ng, wo, w1, w2)


def _odd_kernel(rows_len, x_ref, mod_ref, ngm_ref, ngf_ref, gfin_ref, wi_ref, ws_ref, gb_ref,
                cw_ref, wo_ref, w1_ref, w2_ref, o_ref):
    x = x_ref[0]
    mod = mod_ref[0]
    h = _rms_mod(x, ngm_ref[...], mod[1:2], mod[0:1]).astype(BF)
    proj = _dot(h, wi_ref[...])
    u = _gelu_tanh(proj[:, 0:GW])
    v = _gelu_tanh(proj[:, GW:2 * GW]).astype(BF)
    gate_b = proj[:, 2 * GW:2 * GW + CW]
    z = proj[:, 2 * GW + CW:2 * GW + 2 * CW] * proj[:, 2 * GW + 2 * CW:2 * GW + 3 * CW]

    sp_rows = []
    for c in range(TM // GCH):
        cols = []
        for gi in range(GW // GCH):
            vg = v[c * GCH:(c + 1) * GCH, gi * GCH:(gi + 1) * GCH]
            cols.append(_dot(ws_ref[gi], vg))
        sp_rows.append(jnp.concatenate(cols, axis=1) + gb_ref[...])
    out_c = u * jnp.concatenate(sp_rows, axis=0)

    pos = lax.broadcasted_iota(jnp.int32, (TM, CW), 0) & (rows_len - 1)
    zl = jnp.where(pos >= 1, pltpu.roll(z, 1, axis=0), 0.0)
    zr = jnp.where(pos < rows_len - 1, pltpu.roll(z, TM - 1, axis=0), 0.0)
    cw = cw_ref[...]
    out_d = gate_b * (zl * cw[0:1] + z * cw[1:2] + zr * cw[2:3])

    mix = jnp.concatenate([out_c, out_d], axis=1).astype(BF)
    x1 = x + mod[2:3] * _dot(mix, wo_ref[...])
    x2 = _ffn(x1, mod, ngf_ref[...], w1_ref, w2_ref)
    ms = jnp.mean(x2 * x2, axis=-1, keepdims=True)
    o_ref[0] = x2 * lax.rsqrt(ms + EPS) * gfin_ref[...]


def _odd(x, mods, rows, rows_len, ngm, ngf, gfin, wi, ws, gbias, cw, wo, w1, w2):
    b, l, _ = x.shape
    nt = l // TM
    assert TM % rows_len == 0 and rows_len & (rows_len - 1) == 0 and l % TM == 0
    tok = lambda w: pl.BlockSpec((1, TM, w), lambda i, j: (i, j, 0))
    const = lambda shape: pl.BlockSpec(shape, lambda i, j: tuple(0 for _ in shape))
    return pl.pallas_call(
        functools.partial(_odd_kernel, rows_len),
        grid=(b, nt),
        in_specs=[tok(D), pl.BlockSpec((1, N_MOD, D), lambda i, j: (rows(i), 0, 0)),
                  const((1, D)), const((1, D)), const((1, D)), const((D, OD_COLS)),
                  const((GW // GCH, GCH, GCH)), const((GCH, GW)), const((3, CW)),
                  const((D, D)), const((D, D_FF)), const((D_FF, D))],
        out_specs=tok(D),
        out_shape=jax.ShapeDtypeStruct((b, l, D), F32),
        compiler_params=pltpu.CompilerParams(
            dimension_semantics=("arbitrary", "arbitrary"), vmem_limit_bytes=VMEM_LIMIT),
        name="odd_ffn_final",
    )(x, mods, ngm, ngf, gfin, wi, ws, gbias, cw, wo, w1, w2)


def _pad_heads(w, axis):
    shape = list(w.shape)
    shape[axis:axis + 1] = [H, DV]
    w = w.reshape(shape)
    pad = [(0, 0)] * w.ndim
    pad[axis + 1] = (0, DVP - DV)
    w = jnp.pad(w, pad)
    shape[axis:axis + 2] = [H * DVP]
    return w.reshape(shape)


def _pair_heads(f, b):
    lead = f.shape[:-1]
    fb = jnp.stack([f.reshape(lead + (H, DK)), b.reshape(lead + (H, DK))], axis=-2)
    return fb.reshape(lead + (2 * H * DK,))


def _dft_consts(l):
    n = np.arange(l)
    ang = 2.0 * np.pi * ((n[:, None] * n[None, :]) % l) / l
    cl = np.cos(ang) / np.sqrt(l)
    sl = -np.sin(ang) / np.sqrt(l)
    m = np.arange(FG)
    angc = 2.0 * np.pi * ((m[:, None] * m[None, :]) % FG) / FG
    eye = np.eye(FW // FG)
    cc = np.kron(eye, np.cos(angc)) / np.sqrt(FG)
    sc = np.kron(eye, np.sin(angc)) / np.sqrt(FG)
    to_bf = lambda a: jnp.asarray(a, F32).astype(BF)
    return to_bf(cl), to_bf(sl), to_bf(np.concatenate([cc, sc], axis=1))


def _state_in(s):
    s = jnp.swapaxes(s.astype(F32), -1, -2)
    return jnp.pad(s, ((0, 0), (0, 0), (0, DVP - DV), (0, 0)))


def _state_out(s):
    return jnp.swapaxes(s[:, :, :DV, :], -1, -2)[:, None]


def kernel(x_prompt, x_sample, state_gla_fwd, state_gla_bwd, c, c_ctx, ada_w, ada_b, norm_mix_g,
           norm_ffn_g, ffn_w1, ffn_w2, ev_w_in, ev_w_out, gla_w2_f, gla_b2_f, gla_w2_b, gla_b2_b,
           gla_norm_g, od_w_in, od_w_out, gmlp_ws, gmlp_b, conv_w, final_norm_g):
    c0 = H * DK

    wi0 = ev_w_in[0]
    o = 2 * c0
    wqk = wi0[:, :o].astype(BF)
    wvg = jnp.concatenate([_pad_heads(wi0[:, o:o + H * DV], 1),
                           _pad_heads(wi0[:, o + H * DV:o + 2 * H * DV], 1)], axis=1).astype(BF)
    o += 2 * H * DV
    wlr = jnp.pad(wi0[:, o:o + 2 * GLA_LOWRANK], ((0, 0), (0, LRP - 2 * GLA_LOWRANK))).astype(BF)
    wfin = wi0[:, o + 2 * GLA_LOWRANK:].astype(BF)
    zlr = jnp.zeros((GLA_LOWRANK, c0), F32)
    w2cat = jnp.concatenate([_pair_heads(gla_w2_f[0], zlr), _pair_heads(zlr, gla_w2_b[0])], axis=0)
    w2cat = jnp.pad(w2cat, ((0, LRP - 2 * GLA_LOWRANK), (0, 0))).astype(BF)
    b2cat = _pair_heads(gla_b2_f[0], gla_b2_b[0])[None, :]
    wo0 = ev_w_out[0]
    w_out0 = jnp.concatenate([_pad_heads(wo0[:H * DV], 0), wo0[H * DV:]], axis=0).astype(BF)
    gn = jnp.pad(gla_norm_g[0], (0, DVP - DV))[None, :]
    w1 = ffn_w1.astype(BF)
    w2 = ffn_w2.astype(BF)
    wi1 = od_w_in[0].astype(BF)
    wo1 = od_w_out[0].astype(BF)
    ws = gmlp_ws[0].astype(BF)
    gbias = jnp.repeat(gmlp_b[0].T, GCH, axis=1)
    cw = conv_w[0]
    ngm = norm_mix_g[:, None, :]
    ngf = norm_ffn_g[:, None, :]
    gfin = final_norm_g[None, :]

    cond = jnp.concatenate([c_ctx[None, :], c], axis=0)
    cond = jnp.pad(cond, ((0, 16 - cond.shape[0]), (0, 0)))
    mods = _modulation(cond, ada_w, ada_b).reshape(ada_w.shape[0], 16, N_MOD, D)

    def trunk(x, rows, rows_len, s0f, s0b, has_state):
        l = x.shape[1]
        cl, sl, dftc = _dft_consts(l)
        q, k, la, v, g, xcs = _ev_in(x, mods[0], rows, ngm[0], wlr, wqk, wvg, wfin, w2cat, b2cat, dftc)
        mix, sf, sb = _gla(l, q, k, la, v, g, xcs, s0f, s0b, has_state, gn, cl, sl)
        x = _ev_out(x, mix, mods[0], rows, ngf[0], w_out0, w1[0], w2[0])
        y = _odd(x, mods[1], rows, rows_len, ngm[1], ngf[1], gfin, wi1, ws, gbias, cw, wo1,
                 w1[1], w2[1])
        return y, sf, sb

    zero_state = jnp.zeros((1, H, DVP, DK), F32)
    y_prompt, sf, sb = trunk(x_prompt, lambda i: 0, x_prompt.shape[1], zero_state, zero_state, False)
    y_sample, _, _ = trunk(x_sample, lambda i: i + 1, GRID_W,
                           _state_in(state_gla_fwd[:, 0]), _state_in(state_gla_bwd[:, 0]), True)
    return (y_prompt, y_sample, _state_out(sf), _state_out(sb))
```

```python
import functools

import jax
import jax.numpy as jnp
import numpy as np
from jax import lax
from jax.experimental import pallas as pl
from jax.experimental.pallas import tpu as pltpu

D = 1024
D_FF = 4 * D
EPS = 1e-6
N_MOD = 6
GRID_W = 64

H = 4
DK = 128
DV = 192
DVP = 256
GLA_LOWRANK = 16
GLA_TAU = 16.0
CH = 64
FW = 256
FG = 64
GW = 512
GCH = 128
CW = 512

TM = 256
LRP = 128
MIX0 = H * DVP + FW
OD_COLS = 2 * GW + 3 * CW
GLA_UNROLL = 8
NPREP = 16
N_COND = 16

VMEM_LIMIT = 56 * 1024 * 1024

BF = jnp.bfloat16
F32 = jnp.float32


def _dot(a, b):
    return jnp.dot(a, b, preferred_element_type=F32)


def _dot_nt(a, b):
    return lax.dot_general(a, b, (((1,), (1,)), ((), ())), preferred_element_type=F32)


def _dot_tn(a, b):
    return lax.dot_general(a, b, (((0,), (0,)), ((), ())), preferred_element_type=F32)


def _sigmoid(x):
    return 1.0 / (1.0 + jnp.exp(-x))


def _silu(x):
    return x * _sigmoid(x)


def _log_sigmoid(x):
    return jnp.minimum(x, 0.0) - jnp.log(1.0 + jnp.exp(-jnp.abs(x)))


def _gelu_tanh(x):
    c = np.float32(np.sqrt(2.0 / np.pi))
    return 0.5 * x * (1.0 + jnp.tanh(c * (x + 0.044715 * (x * x * x))))


def _rms_mod(x, g, sc, sh):
    ms = jnp.mean(x * x, axis=-1, keepdims=True)
    return (x * lax.rsqrt(ms + EPS)) * (g * (1.0 + sc)) + sh


class _Tiles:
    def __init__(self, n_ctx, n_smp, tiles_per_smp_seq, lead=0):
        self.n_ctx, self.n_smp, self.tps, self.lead = n_ctx, n_smp, tiles_per_smp_seq, lead
        self.steps = lead + n_ctx + n_smp

    def tile(self, s):
        return jnp.maximum(s - self.lead, 0)

    def ctx(self, s):
        return jnp.minimum(self.tile(s), self.n_ctx - 1)

    def smp(self, s):
        return jnp.maximum(self.tile(s) - self.n_ctx, 0)

    def cond_row(self, s):
        return jnp.where(self.tile(s) < self.n_ctx, 0, 1 + self.smp(s) // self.tps)

    def specs(self, width, dual):
        if dual:
            return [pl.BlockSpec((TM, width), lambda s: (self.ctx(s), 0)),
                    pl.BlockSpec((TM, width), lambda s: (self.smp(s), 0))]
        return [pl.BlockSpec((TM, width), lambda s: (self.tile(s), 0))]

    def mod_spec(self):
        return pl.BlockSpec((1, N_MOD, D), lambda s: (self.cond_row(s), 0, 0))

    def const(self, shape):
        return pl.BlockSpec(shape, lambda s: tuple(0 for _ in shape))

    def chunk(self, layer, rows, cols):
        return pl.BlockSpec((1, rows, cols), lambda s: (layer, jnp.minimum(s, self.lead - 1), 0))


def _cast_chunk(s, src_ref, dst_ref):
    rows = src_ref.shape[1]
    r0 = pl.multiple_of(s * rows, rows)
    dst_ref[pl.ds(r0, rows), :] = src_ref[0].astype(BF)


def _mod_kernel(c_ref, w_ref, b_ref, o_ref):
    a = _silu(c_ref[...]).astype(BF)
    o_ref[0] = _dot(a, w_ref[0].astype(BF)) + b_ref[0]


def _modulation(cond, ada_w, ada_b):
    depth = ada_w.shape[0]
    r = cond.shape[0]
    tn = 1024
    return pl.pallas_call(
        _mod_kernel,
        grid=(depth, N_MOD * D // tn),
        in_specs=[
            pl.BlockSpec((r, D), lambda l, j: (0, 0)),
            pl.BlockSpec((1, D, tn), lambda l, j: (l, 0, j)),
            pl.BlockSpec((1, 1, tn), lambda l, j: (l, 0, j)),
        ],
        out_specs=pl.BlockSpec((1, r, tn), lambda l, j: (l, 0, j)),
        out_shape=jax.ShapeDtypeStruct((depth, r, N_MOD * D), F32),
        compiler_params=pltpu.CompilerParams(
            dimension_semantics=("arbitrary", "arbitrary"), vmem_limit_bytes=VMEM_LIMIT),
        name="modulation",
    )(cond, ada_w, ada_b.reshape(depth, 1, N_MOD * D))


def _ev_in_kernel(n_ctx, xc_ref, xs_ref, mod_ref, ng_ref, wlr_ref, wqk_ref, wvg_ref, wfin_ref,
                  w2_ref, b2_ref, dft_ref, q_ref, k_ref, la_ref, v_ref, g_ref, xcs_ref):
    x = jnp.where(pl.program_id(0) < n_ctx, xc_ref[...], xs_ref[...])
    mod = mod_ref[0]
    h = _rms_mod(x, ng_ref[...], mod[1:2], mod[0:1]).astype(BF)
    c0 = H * DK

    lr = _dot(h, wlr_ref[...]).astype(BF)
    pre = _dot(lr, w2_ref[...]) + b2_ref[...]
    la_ref[...] = _log_sigmoid(pre) * np.float32(1.0 / GLA_TAU)

    qk = _dot(h, wqk_ref[...])
    q_ref[...] = (qk[:, 0:c0] * np.float32(DK ** -0.5)).astype(BF)
    k_ref[...] = qk[:, c0:2 * c0].astype(BF)
    vg = _dot(h, wvg_ref[...])
    v_ref[...] = vg[:, 0:H * DVP].astype(BF)
    g_ref[...] = vg[:, H * DVP:2 * H * DVP].astype(BF)
    fin = _dot(h, wfin_ref[...]).astype(BF)
    xcs_ref[...] = _dot(fin, dft_ref[...]).astype(BF)


def _ev_in(tl, xc, xs, mods, ng, wlr, wqk, wvg, wfin, w2cat, b2cat, dftc):
    c0 = H * DK
    ntok = (tl.n_ctx + tl.n_smp) * TM
    widths = [(c0, BF), (c0, BF), (2 * c0, F32), (H * DVP, BF), (H * DVP, BF), (2 * FW, BF)]
    return pl.pallas_call(
        functools.partial(_ev_in_kernel, tl.n_ctx),
        grid=(tl.steps,),
        in_specs=tl.specs(D, True) + [
            tl.mod_spec(), tl.const((1, D)), tl.const((D, LRP)), tl.const((D, 2 * c0)),
            tl.const((D, 2 * H * DVP)), tl.const((D, FW)), tl.const((LRP, 2 * c0)),
            tl.const((1, 2 * c0)), tl.const((FW, 2 * FW))],
        out_specs=[tl.specs(w, False)[0] for w, _ in widths],
        out_shape=[jax.ShapeDtypeStruct((ntok, w), dt) for w, dt in widths],
        compiler_params=pltpu.CompilerParams(
            dimension_semantics=("arbitrary",), vmem_limit_bytes=VMEM_LIMIT),
        name="ev_in",
    )(xc, xs, mods, ng, wlr, wqk, wvg, wfin, w2cat, b2cat, dftc)


def _chunk_scan(x, pos, reverse):
    s = 1
    while s < CH:
        if reverse:
            y = pltpu.roll(x, CH - s, axis=0)
            x = x + jnp.where(pos < CH - s, y, 0.0)
        else:
            y = pltpu.roll(x, s, axis=0)
            x = x + jnp.where(pos >= s, y, 0.0)
        s *= 2
    return x


def _gla_kernel(l, q_ref, k_ref, la_ref, v_ref, g_ref, xcs_ref, s0f_ref, s0b_ref, gn_ref,
                cl_ref, sl_ref, mix_ref, sf_ref, sb_ref,
                bc_ref, ke_ref, dec_ref, u_ref, sp_ref, st_ref):
    nchunks = l // CH
    unroll = min(GLA_UNROLL, nchunks)
    pos = lax.broadcasted_iota(jnp.int32, (CH, DK), 0)
    row = lax.broadcasted_iota(jnp.int32, (CH, CH), 0)
    col = lax.broadcasted_iota(jnp.int32, (CH, CH), 1)
    gn = gn_ref[...]

    for hh in range(H):
        ks = slice(hh * DK, (hh + 1) * DK)
        vs = slice(hh * DVP, (hh + 1) * DVP)

        def inc_body(n, carry):
            r0 = pl.multiple_of(n * CH, CH)
            la = la_ref[pl.ds(r0, CH), vs]
            bf = _chunk_scan(la[:, 0:DK], pos, False)
            bb = _chunk_scan(la[:, DK:2 * DK], pos, True)
            bc_ref[pl.ds(r0, CH), 0:DK] = bf
            bc_ref[pl.ds(r0, CH), DK:2 * DK] = bb
            dec_ref[n, :, 0:DK] = jnp.exp(bf[CH - 1:CH, :])
            dec_ref[n, :, DK:2 * DK] = jnp.exp(bb[0:1, :])
            kk = k_ref[pl.ds(r0, CH), ks].astype(F32)
            ke = jnp.concatenate([kk * jnp.exp(-bf), kk * jnp.exp(-bb)], axis=1).astype(BF)
            ke_ref[pl.ds(r0, CH), :] = ke
            u_ref[n] = _dot_tn(v_ref[pl.ds(r0, CH), vs], ke)
            return carry

        lax.fori_loop(0, nchunks, inc_body, 0, unroll=unroll)

        st_ref[:, 0:DK] = s0f_ref[0, hh]
        st_ref[:, DK:2 * DK] = s0b_ref[0, hh]

        def scan_body(i, carry):
            nf = i
            nb = nchunks - 1 - i
            sf = st_ref[:, 0:DK]
            sb = st_ref[:, DK:2 * DK]
            sp_ref[nf, :, 0:DK] = sf.astype(BF)
            sp_ref[nb, :, DK:2 * DK] = sb.astype(BF)
            st_ref[:, 0:DK] = dec_ref[nf, :, 0:DK] * (sf + u_ref[nf, :, 0:DK])
            st_ref[:, DK:2 * DK] = dec_ref[nb, :, DK:2 * DK] * (sb + u_ref[nb, :, DK:2 * DK])
            return carry

        lax.fori_loop(0, nchunks, scan_body, 0)
        sf_ref[0, hh] = st_ref[:, 0:DK]
        sb_ref[0, hh] = st_ref[:, DK:2 * DK]

        def out_body(n, carry):
            r0 = pl.multiple_of(n * CH, CH)
            bc = bc_ref[pl.ds(r0, CH), :]
            qq = q_ref[pl.ds(r0, CH), ks].astype(F32)
            qe = jnp.concatenate([qq * jnp.exp(bc[:, 0:DK]), qq * jnp.exp(bc[:, DK:2 * DK])],
                                 axis=1).astype(BF)
            ke = ke_ref[pl.ds(r0, CH), :]
            att = (jnp.where(row >= col, _dot_nt(qe[:, 0:DK], ke[:, 0:DK]), 0.0)
                   + jnp.where(row <= col, _dot_nt(qe[:, DK:], ke[:, DK:]), 0.0)).astype(BF)
            o = _dot(att, v_ref[pl.ds(r0, CH), vs]) + _dot_nt(qe, sp_ref[n])
            ms = jnp.sum(o * o, axis=-1, keepdims=True) * np.float32(1.0 / DV)
            on = o * lax.rsqrt(ms + EPS) * gn
            gate = _silu(g_ref[pl.ds(r0, CH), vs].astype(F32))
            mix_ref[pl.ds(r0, CH), vs] = (on * gate).astype(BF)
            return carry

        lax.fori_loop(0, nchunks, out_body, 0, unroll=unroll)

    fo = _dot(cl_ref[...], xcs_ref[:, 0:FW]) + _dot(sl_ref[...], xcs_ref[:, FW:2 * FW])
    mix_ref[:, H * DVP:H * DVP + FW] = fo.astype(BF)


def _gla(b, l, tok0, q, k, la, v, g, xcs, s0f, s0b, has_state, gn, cl, sl):
    nchunks = l // CH
    c0 = H * DK
    assert tok0 % l == 0
    seq = lambda w: pl.BlockSpec((l, w), lambda i: (tok0 // l + i, 0))
    st_idx = (lambda i: (i, 0, 0, 0)) if has_state else (lambda i: (0, 0, 0, 0))
    st_in = pl.BlockSpec((1, H, DVP, DK), st_idx)
    st_out = pl.BlockSpec((1, H, DVP, DK), lambda i: (i, 0, 0, 0))
    const = lambda shape: pl.BlockSpec(shape, lambda i: tuple(0 for _ in shape))
    st_shape = jax.ShapeDtypeStruct((b, H, DVP, DK), F32)
    return pl.pallas_call(
        functools.partial(_gla_kernel, l),
        grid=(b,),
        in_specs=[seq(c0), seq(c0), seq(2 * c0), seq(H * DVP), seq(H * DVP), seq(2 * FW),
                  st_in, st_in, const((1, DVP)), const((l, l)), const((l, l))],
        out_specs=[pl.BlockSpec((l, MIX0), lambda i: (i, 0)), st_out, st_out],
        out_shape=[jax.ShapeDtypeStruct((b * l, MIX0), BF), st_shape, st_shape],
        scratch_shapes=[pltpu.VMEM((l, 2 * DK), F32),
                        pltpu.VMEM((l, 2 * DK), BF),
                        pltpu.VMEM((nchunks, 1, 2 * DK), F32),
                        pltpu.VMEM((nchunks, DVP, 2 * DK), F32),
                        pltpu.VMEM((nchunks, DVP, 2 * DK), BF),
                        pltpu.VMEM((DVP, 2 * DK), F32)],
        compiler_params=pltpu.CompilerParams(
            dimension_semantics=("arbitrary",), vmem_limit_bytes=VMEM_LIMIT),
        name="gla_fnet",
    )(q, k, la, v, g, xcs, s0f, s0b, gn, cl, sl)


def _ffn(x1, mod, ng, w1_ref, w2_ref):
    h = _rms_mod(x1, ng, mod[4:5], mod[3:4]).astype(BF)
    hid = jnp.maximum(_dot(h, w1_ref[...]), 0.0)
    hid = (hid * hid).astype(BF)
    return x1 + mod[5:6] * _dot(hid, w2_ref[...])


def _ev_out_kernel(tl, xc_ref, xs_ref, mc_ref, ms_ref, mod_ref, ng_ref, wo_ref, w1c_ref, w2c_ref,
                   o_ref, w1_ref, w2_ref):
    s = pl.program_id(0)

    @pl.when(s < tl.lead)
    def _cast_weights():
        _cast_chunk(s, w1c_ref, w1_ref)
        _cast_chunk(s, w2c_ref, w2_ref)

    @pl.when(s >= tl.lead)
    def _tile():
        is_ctx = s - tl.lead < tl.n_ctx
        mod = mod_ref[0]
        x = jnp.where(is_ctx, xc_ref[...], xs_ref[...])
        mix = jnp.where(is_ctx, mc_ref[...], ms_ref[...])
        x1 = x + mod[2:3] * _dot(mix, wo_ref[...])
        o_ref[...] = _ffn(x1, mod, ng_ref[...], w1_ref, w2_ref)


def _ev_out(tl, xc, xs, mix_c, mix_s, mods, ng, wo, ffn_w1, ffn_w2, layer):
    ntok = (tl.n_ctx + tl.n_smp) * TM
    return pl.pallas_call(
        functools.partial(_ev_out_kernel, tl),
        grid=(tl.steps,),
        in_specs=tl.specs(D, True) + tl.specs(MIX0, True) + [
            tl.mod_spec(), tl.const((1, D)), tl.const((MIX0, D)),
            tl.chunk(layer, D // tl.lead, D_FF), tl.chunk(layer, D_FF // tl.lead, D)],
        out_specs=tl.specs(D, False)[0],
        out_shape=jax.ShapeDtypeStruct((ntok, D), F32),
        scratch_shapes=[pltpu.VMEM((D, D_FF), BF), pltpu.VMEM((D_FF, D), BF)],
        compiler_params=pltpu.CompilerParams(
            dimension_semantics=("arbitrary",), vmem_limit_bytes=VMEM_LIMIT),
        name="ev_out_ffn",
    )(xc, xs, mix_c, mix_s, mods, ng, wo, ffn_w1, ffn_w2)


def _odd_kernel(tl, ctx_rows, smp_rows, x_ref, mod_ref, ngm_ref, ngf_ref, gfin_ref, ws_ref, gb_ref,
                cw_ref, wic_ref, woc_ref, w1c_ref, w2c_ref, yc_ref, ys_ref,
                wi_ref, wo_ref, w1_ref, w2_ref):
    s = pl.program_id(0)

    @pl.when(s < tl.lead)
    def _cast_weights():
        _cast_chunk(s, wic_ref, wi_ref)
        _cast_chunk(s, woc_ref, wo_ref)
        _cast_chunk(s, w1c_ref, w1_ref)
        _cast_chunk(s, w2c_ref, w2_ref)

    @pl.when(s >= tl.lead)
    def _tile():
        is_ctx = s - tl.lead < tl.n_ctx
        x = x_ref[...]
        mod = mod_ref[0]
        h = _rms_mod(x, ngm_ref[...], mod[1:2], mod[0:1]).astype(BF)
        proj = _dot(h, wi_ref[...])
        u = _gelu_tanh(proj[:, 0:GW])
        v = _gelu_tanh(proj[:, GW:2 * GW]).astype(BF)
        gate_b = proj[:, 2 * GW:2 * GW + CW]
        z = proj[:, 2 * GW + CW:2 * GW + 2 * CW] * proj[:, 2 * GW + 2 * CW:2 * GW + 3 * CW]

        sp_rows = []
        for c in range(TM // GCH):
            cols = []
            for gi in range(GW // GCH):
                vg = v[c * GCH:(c + 1) * GCH, gi * GCH:(gi + 1) * GCH]
                cols.append(_dot(ws_ref[gi], vg))
            sp_rows.append(jnp.concatenate(cols, axis=1) + gb_ref[...])
        out_c = u * jnp.concatenate(sp_rows, axis=0)

        last = jnp.where(is_ctx, ctx_rows - 1, smp_rows - 1)
        pos = lax.broadcasted_iota(jnp.int32, (TM, CW), 0) & last
        zl = jnp.where(pos >= 1, pltpu.roll(z, 1, axis=0), 0.0)
        zr = jnp.where(pos < last, pltpu.roll(z, TM - 1, axis=0), 0.0)
        cw = cw_ref[...]
        out_d = gate_b * (zl * cw[0:1] + z * cw[1:2] + zr * cw[2:3])

        mix = jnp.concatenate([out_c, out_d], axis=1).astype(BF)
        x1 = x + mod[2:3] * _dot(mix, wo_ref[...])
        x2 = _ffn(x1, mod, ngf_ref[...], w1_ref, w2_ref)
        ms = jnp.mean(x2 * x2, axis=-1, keepdims=True)
        y = x2 * lax.rsqrt(ms + EPS) * gfin_ref[...]

        @pl.when(is_ctx)
        def _store_ctx():
            yc_ref[...] = y

        @pl.when(jnp.logical_not(is_ctx))
        def _store_smp():
            ys_ref[...] = y


def _odd(tl, ctx_rows, smp_rows, x, mods, ngm, ngf, gfin, ws, gbias, cw, od_w_in, od_w_out,
         ffn_w1, ffn_w2, layer):
    for r in (ctx_rows, smp_rows):
        assert TM % r == 0 and r & (r - 1) == 0
    ys = tl.specs(D, True)
    return pl.pallas_call(
        functools.partial(_odd_kernel, tl, ctx_rows, smp_rows),
        grid=(tl.steps,),
        in_specs=tl.specs(D, False) + [
            tl.mod_spec(), tl.const((1, D)), tl.const((1, D)), tl.const((1, D)),
            tl.const((GW // GCH, GCH, GCH)), tl.const((GCH, GW)), tl.const((3, CW)),
            tl.chunk(0, D // tl.lead, OD_COLS), tl.chunk(0, D // tl.lead, D),
            tl.chunk(layer, D // tl.lead, D_FF), tl.chunk(layer, D_FF // tl.lead, D)],
        out_specs=ys,
        out_shape=[jax.ShapeDtypeStruct((tl.n_ctx * TM, D), F32),
                   jax.ShapeDtypeStruct((tl.n_smp * TM, D), F32)],
        scratch_shapes=[pltpu.VMEM((D, OD_COLS), BF), pltpu.VMEM((D, D), BF),
                        pltpu.VMEM((D, D_FF), BF), pltpu.VMEM((D_FF, D), BF)],
        compiler_params=pltpu.CompilerParams(
            dimension_semantics=("arbitrary",), vmem_limit_bytes=VMEM_LIMIT),
        name="odd_ffn_final",
    )(x, mods, ngm, ngf, gfin, ws, gbias, cw, od_w_in, od_w_out, ffn_w1, ffn_w2)


def _pad_heads(w, axis):
    shape = list(w.shape)
    shape[axis:axis + 1] = [H, DV]
    w = w.reshape(shape)
    pad = [(0, 0)] * w.ndim
    pad[axis + 1] = (0, DVP - DV)
    w = jnp.pad(w, pad)
    shape[axis:axis + 2] = [H * DVP]
    return w.reshape(shape)


def _pair_heads(f, b):
    lead = f.shape[:-1]
    fb = jnp.stack([f.reshape(lead + (H, DK)), b.reshape(lead + (H, DK))], axis=-2)
    return fb.reshape(lead + (2 * H * DK,))


def _dft_pos(l):
    n = np.arange(l)
    ang = 2.0 * np.pi * ((n[:, None] * n[None, :]) % l) / l
    to_bf = lambda a: jnp.asarray(a, F32).astype(BF)
    return to_bf(np.cos(ang) / np.sqrt(l)), to_bf(-np.sin(ang) / np.sqrt(l))


def _dft_chan():
    m = np.arange(FG)
    angc = 2.0 * np.pi * ((m[:, None] * m[None, :]) % FG) / FG
    eye = np.eye(FW // FG)
    cc = np.kron(eye, np.cos(angc)) / np.sqrt(FG)
    sc = np.kron(eye, np.sin(angc)) / np.sqrt(FG)
    return jnp.asarray(np.concatenate([cc, sc], axis=1), F32).astype(BF)


def _state_in(s):
    s = jnp.swapaxes(s.astype(F32), -1, -2)
    return jnp.pad(s, ((0, 0), (0, 0), (0, DVP - DV), (0, 0)))


def _state_out(s):
    return jnp.swapaxes(s[:, :, :DV, :], -1, -2)[:, None]


def kernel(x_prompt, x_sample, state_gla_fwd, state_gla_bwd, c, c_ctx, ada_w, ada_b, norm_mix_g,
           norm_ffn_g, ffn_w1, ffn_w2, ev_w_in, ev_w_out, gla_w2_f, gla_b2_f, gla_w2_b, gla_b2_b,
           gla_norm_g, od_w_in, od_w_out, gmlp_ws, gmlp_b, conv_w, final_norm_g):
    c0 = H * DK
    b_ctx, l_ctx, _ = x_prompt.shape
    b_smp, l_smp, _ = x_sample.shape
    assert l_ctx == TM and l_smp % TM == 0 and 1 + b_smp <= N_COND
    n_ctx, n_smp, tps = b_ctx * l_ctx // TM, b_smp * l_smp // TM, l_smp // TM
    tl = _Tiles(n_ctx, n_smp, tps)
    tl_cast = _Tiles(n_ctx, n_smp, tps, lead=NPREP)

    wi0 = ev_w_in[0]
    o = 2 * c0
    wqk = wi0[:, :o].astype(BF)
    wvg = jnp.concatenate([_pad_heads(wi0[:, o:o + H * DV], 1),
                           _pad_heads(wi0[:, o + H * DV:o + 2 * H * DV], 1)], axis=1).astype(BF)
    o += 2 * H * DV
    wlr = jnp.pad(wi0[:, o:o + 2 * GLA_LOWRANK], ((0, 0), (0, LRP - 2 * GLA_LOWRANK))).astype(BF)
    wfin = wi0[:, o + 2 * GLA_LOWRANK:].astype(BF)
    zlr = jnp.zeros((GLA_LOWRANK, c0), F32)
    w2cat = jnp.concatenate([_pair_heads(gla_w2_f[0], zlr), _pair_heads(zlr, gla_w2_b[0])], axis=0)
    w2cat = jnp.pad(w2cat, ((0, LRP - 2 * GLA_LOWRANK), (0, 0))).astype(BF)
    b2cat = _pair_heads(gla_b2_f[0], gla_b2_b[0])[None, :]
    wo0 = ev_w_out[0]
    w_out0 = jnp.concatenate([_pad_heads(wo0[:H * DV], 0), wo0[H * DV:]], axis=0).astype(BF)
    gn = jnp.pad(gla_norm_g[0], (0, DVP - DV))[None, :]
    ws = gmlp_ws[0].astype(BF)
    gbias = jnp.repeat(gmlp_b[0].T, GCH, axis=1)
    ngm = norm_mix_g[:, None, :]
    ngf = norm_ffn_g[:, None, :]
    gfin = final_norm_g[None, :]

    cond = jnp.concatenate([c_ctx[None, :], c], axis=0)
    cond = jnp.pad(cond, ((0, N_COND - cond.shape[0]), (0, 0)))
    mods = _modulation(cond, ada_w, ada_b).reshape(ada_w.shape[0], N_COND, N_MOD, D)

    xc = x_prompt.reshape(b_ctx * l_ctx, D)
    xs = x_sample.reshape(b_smp * l_smp, D)

    q, k, la, v, g, xcs = _ev_in(tl, xc, xs, mods[0], ngm[0], wlr, wqk, wvg, wfin, w2cat, b2cat,
                                 _dft_chan())
    zero_state = jnp.zeros((1, H, DVP, DK), F32)
    mix_c, sf, sb = _gla(b_ctx, l_ctx, 0, q, k, la, v, g, xcs, zero_state, zero_state, False, gn,
                         *_dft_pos(l_ctx))
    mix_s, _, _ = _gla(b_smp, l_smp, b_ctx * l_ctx, q, k, la, v, g, xcs,
                       _state_in(state_gla_fwd[:, 0]), _state_in(state_gla_bwd[:, 0]), True, gn,
                       *_dft_pos(l_smp))
    x1 = _ev_out(tl_cast, xc, xs, mix_c, mix_s, mods[0], ngf[0], w_out0, ffn_w1, ffn_w2, 0)

    yc, ys = _odd(tl_cast, l_ctx, GRID_W, x1, mods[1], ngm[1], ngf[1], gfin, ws, gbias, conv_w[0],
                  od_w_in, od_w_out, ffn_w1, ffn_w2, 1)
    return (yc.reshape(b_ctx, l_ctx, D), ys.reshape(b_smp, l_smp, D), _state_out(sf), _state_out(sb))
```

```python
import functools

import jax
import jax.numpy as jnp
import numpy as np
from jax import lax
from jax.experimental import pallas as pl
from jax.experimental.pallas import tpu as pltpu

D = 1024
D_FF = 4 * D
EPS = 1e-6
N_MOD = 6
GRID_W = 64

H = 4
DK = 128
DV = 192
DVP = 256
GLA_LOWRANK = 16
GLA_TAU = 16.0
CH = 64
FW = 256
FG = 64
GW = 512
GCH = 128
CW = 512

EV_V0 = 2 * H * DK
EV_G0 = EV_V0 + H * DV
EV_LR0 = EV_G0 + H * DV
EV_FIN0 = EV_LR0 + 2 * GLA_LOWRANK
EV_IN = EV_FIN0 + FW

TM = 256
LRP = 128
MIX0 = H * DVP + FW
OD_COLS = 2 * GW + 3 * CW
GLA_UNROLL = 8
NPREP = 16
CROWS = D // NPREP
SUB = 8
LANE = 128

VMEM_LIMIT = 56 * 1024 * 1024

BF = jnp.bfloat16
F32 = jnp.float32


def _dot(a, b):
    return jnp.dot(a, b, preferred_element_type=F32)


def _dot_nt(a, b):
    return lax.dot_general(a, b, (((1,), (1,)), ((), ())), preferred_element_type=F32)


def _dot_tn(a, b):
    return lax.dot_general(a, b, (((0,), (0,)), ((), ())), preferred_element_type=F32)


def _sigmoid(x):
    return 1.0 / (1.0 + jnp.exp(-x))


def _silu(x):
    return x * _sigmoid(x)


def _log_sigmoid(x):
    return jnp.minimum(x, 0.0) - jnp.log(1.0 + jnp.exp(-jnp.abs(x)))


def _gelu_tanh(x):
    c = np.float32(np.sqrt(2.0 / np.pi))
    return 0.5 * x * (1.0 + jnp.tanh(c * (x + 0.044715 * (x * x * x))))


def _rms_mod(x, g, sc, sh):
    ms = jnp.mean(x * x, axis=-1, keepdims=True)
    return (x * lax.rsqrt(ms + EPS)) * (g * (1.0 + sc)) + sh


def _pair_heads(f, b):
    return jnp.concatenate([p for hh in range(H)
                            for p in (f[:, hh * DK:(hh + 1) * DK], b[:, hh * DK:(hh + 1) * DK])], axis=1)


class _Tiles:
    def __init__(self, n_ctx, n_smp, tiles_per_smp_seq, lead):
        self.n_ctx, self.n_smp, self.tps, self.lead = n_ctx, n_smp, tiles_per_smp_seq, lead
        self.steps = lead + n_ctx + n_smp

    def tile(self, s):
        return jnp.maximum(s - self.lead, 0)

    def ctx(self, s):
        return jnp.minimum(self.tile(s), self.n_ctx - 1)

    def smp(self, s):
        return jnp.maximum(self.tile(s) - self.n_ctx, 0)

    def cond_row(self, s):
        return jnp.where(self.tile(s) < self.n_ctx, 0, SUB + self.smp(s) // self.tps)

    def specs(self, width, dual):
        if dual:
            return [pl.BlockSpec((TM, width), lambda s: (self.ctx(s), 0)),
                    pl.BlockSpec((TM, width), lambda s: (self.smp(s), 0))]
        return [pl.BlockSpec((TM, width), lambda s: (self.tile(s), 0))]

    def mod_spec(self, layer):
        return pl.BlockSpec((1, SUB, N_MOD * D), lambda s: (layer, self.cond_row(s) // SUB, 0))

    def mod_vecs(self, s, mod_ref):
        row = mod_ref[0, pl.ds(self.cond_row(s) % SUB, 1), :]
        return [row[:, i * D:(i + 1) * D] for i in range(N_MOD)]

    def whole(self, shape, layer=None):
        if layer is None:
            return pl.BlockSpec(shape, lambda s: tuple(0 for _ in shape))
        return pl.BlockSpec((1,) + shape[1:], lambda s: (layer,) + tuple(0 for _ in shape[1:]))

    def chunk(self, layer, rows, cols):
        return pl.BlockSpec((1, rows, cols), lambda s: (layer, jnp.minimum(s, self.lead - 1), 0))


def _cast_chunk(s, src_ref, dst_ref):
    rows = src_ref.shape[1]
    r0 = pl.multiple_of(s * rows, rows)
    dst_ref[pl.ds(r0, rows), :] = src_ref[0].astype(BF)


def _mod_kernel(cx_ref, c_ref, w_ref, b_ref, o_ref):
    w = w_ref[0].astype(BF)
    ax = jnp.broadcast_to(_silu(cx_ref[...]), (SUB, D)).astype(BF)
    o_ref[0, 0:SUB, :] = _dot(ax, w) + b_ref[0]
    o_ref[0, SUB:, :] = _dot(_silu(c_ref[...]).astype(BF), w) + b_ref[0]


def _modulation(c_ctx, c, ada_w, ada_b):
    depth = ada_w.shape[0]
    r = SUB + c.shape[0]
    tn = 1024
    return pl.pallas_call(
        _mod_kernel,
        grid=(depth, N_MOD * D // tn),
        in_specs=[
            pl.BlockSpec((1, D), lambda l, j: (0, 0)),
            pl.BlockSpec(c.shape, lambda l, j: (0, 0)),
            pl.BlockSpec((1, D, tn), lambda l, j: (l, 0, j)),
            pl.BlockSpec((1, 1, tn), lambda l, j: (l, 0, j)),
        ],
        out_specs=pl.BlockSpec((1, r, tn), lambda l, j: (l, 0, j)),
        out_shape=jax.ShapeDtypeStruct((depth, r, N_MOD * D), F32),
        compiler_params=pltpu.CompilerParams(
            dimension_semantics=("arbitrary", "arbitrary"), vmem_limit_bytes=VMEM_LIMIT),
        name="modulation",
    )(c_ctx.reshape(1, D), c, ada_w, ada_b.reshape(depth, 1, N_MOD * D))


def _ev_in_kernel(tl, xc_ref, xs_ref, mod_ref, ng_ref, wc_ref, w2f_ref, w2b_ref, b2f_ref, b2b_ref,
                  dft_ref, q_ref, k_ref, la_ref, v_ref, g_ref, xcs_ref,
                  wqk_ref, wvg_ref, wlr_ref, wfin_ref, w2_ref):
    s = pl.program_id(0)
    c0 = H * DK

    @pl.when(s == 0)
    def _gate_weights():
        z = jnp.zeros((GLA_LOWRANK, c0), F32)
        w2_ref[...] = jnp.zeros_like(w2_ref)
        w2_ref[0:GLA_LOWRANK, :] = _pair_heads(w2f_ref[0], z).astype(BF)
        w2_ref[GLA_LOWRANK:2 * GLA_LOWRANK, :] = _pair_heads(z, w2b_ref[0]).astype(BF)

    @pl.when(s < D // LANE)
    def _cast_weights():
        w = wc_ref[0]
        r0 = pl.multiple_of(s * LANE, LANE)
        rows = pl.ds(r0, LANE)

        def tr(lo, hi, pad):
            piece = w[lo:hi, :]
            if pad:
                piece = jnp.concatenate([piece, jnp.zeros((pad, LANE), F32)], axis=0)
            return piece.T.astype(BF)

        wqk_ref[rows, :] = tr(0, EV_V0, 0)
        for hh in range(H):
            wvg_ref[rows, hh * DVP:(hh + 1) * DVP] = tr(EV_V0 + hh * DV, EV_V0 + (hh + 1) * DV, DVP - DV)
            wvg_ref[rows, (H + hh) * DVP:(H + hh + 1) * DVP] = tr(
                EV_G0 + hh * DV, EV_G0 + (hh + 1) * DV, DVP - DV)
        wlr_ref[rows, :] = tr(EV_LR0, EV_FIN0, LRP - 2 * GLA_LOWRANK)
        wfin_ref[rows, :] = tr(EV_FIN0, EV_IN, 0)

    @pl.when(s >= tl.lead)
    def _tile():
        x = jnp.where(s - tl.lead < tl.n_ctx, xc_ref[...], xs_ref[...])
        mod = tl.mod_vecs(s, mod_ref)
        h = _rms_mod(x, ng_ref[0], mod[1], mod[0]).astype(BF)

        lr = _dot(h, wlr_ref[...]).astype(BF)
        pre = _dot(lr, w2_ref[...]) + _pair_heads(b2f_ref[...], b2b_ref[...])
        la_ref[...] = _log_sigmoid(pre) * np.float32(1.0 / GLA_TAU)

        qk = _dot(h, wqk_ref[...])
        q_ref[...] = (qk[:, 0:c0] * np.float32(DK ** -0.5)).astype(BF)
        k_ref[...] = qk[:, c0:2 * c0].astype(BF)
        vg = _dot(h, wvg_ref[...])
        v_ref[...] = vg[:, 0:H * DVP].astype(BF)
        g_ref[...] = vg[:, H * DVP:2 * H * DVP].astype(BF)
        fin = _dot(h, wfin_ref[...]).astype(BF)
        xcs_ref[...] = _dot(fin, dft_ref[...]).astype(BF)


def _ev_in(tl, xc, xs, mods, ngm, ev_w_in_t, w2f, w2b, b2f, b2b, dftc):
    c0 = H * DK
    assert tl.lead >= D // LANE
    ntok = (tl.n_ctx + tl.n_smp) * TM
    widths = [(c0, BF), (c0, BF), (2 * c0, F32), (H * DVP, BF), (H * DVP, BF), (2 * FW, BF)]
    return pl.pallas_call(
        functools.partial(_ev_in_kernel, tl),
        grid=(tl.steps,),
        in_specs=tl.specs(D, True) + [
            tl.mod_spec(0), tl.whole(ngm.shape, 0),
            pl.BlockSpec((1, EV_IN, LANE), lambda s: (0, 0, jnp.minimum(s, D // LANE - 1))),
            tl.whole(w2f.shape), tl.whole(w2b.shape), tl.whole(b2f.shape), tl.whole(b2b.shape),
            tl.whole(dftc.shape)],
        out_specs=[tl.specs(w, False)[0] for w, _ in widths],
        out_shape=[jax.ShapeDtypeStruct((ntok, w), dt) for w, dt in widths],
        scratch_shapes=[pltpu.VMEM((D, 2 * c0), BF), pltpu.VMEM((D, 2 * H * DVP), BF),
                        pltpu.VMEM((D, LRP), BF), pltpu.VMEM((D, FW), BF),
                        pltpu.VMEM((LRP, 2 * c0), BF)],
        compiler_params=pltpu.CompilerParams(
            dimension_semantics=("arbitrary",), vmem_limit_bytes=VMEM_LIMIT),
        name="ev_in",
    )(xc, xs, mods, ngm, ev_w_in_t, w2f, w2b, b2f, b2b, dftc)


def _chunk_scan(x, pos, reverse):
    s = 1
    while s < CH:
        if reverse:
            y = pltpu.roll(x, CH - s, axis=0)
            x = x + jnp.where(pos < CH - s, y, 0.0)
        else:
            y = pltpu.roll(x, s, axis=0)
            x = x + jnp.where(pos >= s, y, 0.0)
        s *= 2
    return x


def _gla_kernel(l, has_state, *refs):
    q_ref, k_ref, la_ref, v_ref, g_ref, xcs_ref, gn_ref, cl_ref, sl_ref = refs[:9]
    if has_state:
        s0f_ref, s0b_ref, mix_ref = refs[9:12]
    else:
        mix_ref, sf_ref, sb_ref = refs[9:12]
    bc_ref, ke_ref, dec_ref, u_ref, sp_ref, st_ref = refs[12:]

    nchunks = l // CH
    unroll = min(GLA_UNROLL, nchunks)
    pos = lax.broadcasted_iota(jnp.int32, (CH, DK), 0)
    row = lax.broadcasted_iota(jnp.int32, (CH, CH), 0)
    col = lax.broadcasted_iota(jnp.int32, (CH, CH), 1)
    gn = jnp.concatenate([gn_ref[...], jnp.zeros((1, DVP - DV), F32)], axis=1)

    for hh in range(H):
        ks = slice(hh * DK, (hh + 1) * DK)
        vs = slice(hh * DVP, (hh + 1) * DVP)

        def inc_body(n, carry):
            r0 = pl.multiple_of(n * CH, CH)
            la = la_ref[pl.ds(r0, CH), vs]
            bf = _chunk_scan(la[:, 0:DK], pos, False)
            bb = _chunk_scan(la[:, DK:2 * DK], pos, True)
            bc_ref[pl.ds(r0, CH), 0:DK] = bf
            bc_ref[pl.ds(r0, CH), DK:2 * DK] = bb
            dec_ref[n, :, 0:DK] = jnp.exp(bf[CH - 1:CH, :])
            dec_ref[n, :, DK:2 * DK] = jnp.exp(bb[0:1, :])
            kk = k_ref[pl.ds(r0, CH), ks].astype(F32)
            ke = jnp.concatenate([kk * jnp.exp(-bf), kk * jnp.exp(-bb)], axis=1).astype(BF)
            ke_ref[pl.ds(r0, CH), :] = ke
            u_ref[n] = _dot_tn(v_ref[pl.ds(r0, CH), vs], ke)
            return carry

        lax.fori_loop(0, nchunks, inc_body, 0, unroll=unroll)

        if has_state:
            st_ref[0:DV, 0:DK] = s0f_ref[0, 0, hh]
            st_ref[0:DV, DK:2 * DK] = s0b_ref[0, 0, hh]
            st_ref[DV:DVP, :] = jnp.zeros((DVP - DV, 2 * DK), F32)
        else:
            st_ref[...] = jnp.zeros_like(st_ref)

        def scan_body(i, carry):
            nf = i
            nb = nchunks - 1 - i
            sf = st_ref[:, 0:DK]
            sb = st_ref[:, DK:2 * DK]
            sp_ref[nf, :, 0:DK] = sf.astype(BF)
            sp_ref[nb, :, DK:2 * DK] = sb.astype(BF)
            st_ref[:, 0:DK] = dec_ref[nf, :, 0:DK] * (sf + u_ref[nf, :, 0:DK])
            st_ref[:, DK:2 * DK] = dec_ref[nb, :, DK:2 * DK] * (sb + u_ref[nb, :, DK:2 * DK])
            return carry

        lax.fori_loop(0, nchunks, scan_body, 0)
        if not has_state:
            sf_ref[0, 0, hh] = st_ref[0:DV, 0:DK]
            sb_ref[0, 0, hh] = st_ref[0:DV, DK:2 * DK]

        def out_body(n, carry):
            r0 = pl.multiple_of(n * CH, CH)
            bc = bc_ref[pl.ds(r0, CH), :]
            qq = q_ref[pl.ds(r0, CH), ks].astype(F32)
            qe = jnp.concatenate([qq * jnp.exp(bc[:, 0:DK]), qq * jnp.exp(bc[:, DK:2 * DK])],
                                 axis=1).astype(BF)
            ke = ke_ref[pl.ds(r0, CH), :]
            att = (jnp.where(row >= col, _dot_nt(qe[:, 0:DK], ke[:, 0:DK]), 0.0)
                   + jnp.where(row <= col, _dot_nt(qe[:, DK:], ke[:, DK:]), 0.0)).astype(BF)
            o = _dot(att, v_ref[pl.ds(r0, CH), vs]) + _dot_nt(qe, sp_ref[n])
            ms = jnp.sum(o * o, axis=-1, keepdims=True) * np.float32(1.0 / DV)
            on = o * lax.rsqrt(ms + EPS) * gn
            gate = _silu(g_ref[pl.ds(r0, CH), vs].astype(F32))
            mix_ref[pl.ds(r0, CH), vs] = (on * gate).astype(BF)
            return carry

        lax.fori_loop(0, nchunks, out_body, 0, unroll=unroll)

    fo = _dot(cl_ref[...], xcs_ref[:, 0:FW]) + _dot(sl_ref[...], xcs_ref[:, FW:2 * FW])
    mix_ref[:, H * DVP:H * DVP + FW] = fo.astype(BF)


def _gla(b, l, tok0, q, k, la, v, g, xcs, gn, cl, sl, states):
    nchunks = l // CH
    c0 = H * DK
    assert tok0 % l == 0
    seq = lambda w: pl.BlockSpec((l, w), lambda i: (tok0 // l + i, 0))
    const = lambda shape: pl.BlockSpec(shape, lambda i: tuple(0 for _ in shape))
    st_spec = pl.BlockSpec((1, 1, H, DV, DK), lambda i: (i, 0, 0, 0, 0))
    in_specs = [seq(c0), seq(c0), seq(2 * c0), seq(H * DVP), seq(H * DVP), seq(2 * FW),
                const(gn.shape), const((l, l)), const((l, l))]
    out_specs = [pl.BlockSpec((l, MIX0), lambda i: (i, 0))]
    out_shape = [jax.ShapeDtypeStruct((b * l, MIX0), BF)]
    args = [q, k, la, v, g, xcs, gn, cl, sl]
    if states is not None:
        in_specs += [st_spec, st_spec]
        args += list(states)
    else:
        out_specs += [st_spec, st_spec]
        out_shape += [jax.ShapeDtypeStruct((b, 1, H, DV, DK), F32)] * 2
    return pl.pallas_call(
        functools.partial(_gla_kernel, l, states is not None),
        grid=(b,),
        in_specs=in_specs,
        out_specs=out_specs,
        out_shape=out_shape,
        scratch_shapes=[pltpu.VMEM((l, 2 * DK), F32),
                        pltpu.VMEM((l, 2 * DK), BF),
                        pltpu.VMEM((nchunks, 1, 2 * DK), F32),
                        pltpu.VMEM((nchunks, DVP, 2 * DK), F32),
                        pltpu.VMEM((nchunks, DVP, 2 * DK), BF),
                        pltpu.VMEM((DVP, 2 * DK), F32)],
        compiler_params=pltpu.CompilerParams(
            dimension_semantics=("arbitrary",), vmem_limit_bytes=VMEM_LIMIT),
        name="gla_fnet",
    )(*args)


def _ffn(x1, mod, ng, w1_ref, w2_ref):
    h = _rms_mod(x1, ng, mod[4], mod[3]).astype(BF)
    hid = jnp.maximum(_dot(h, w1_ref[...]), 0.0)
    hid = (hid * hid).astype(BF)
    return x1 + mod[5] * _dot(hid, w2_ref[...])


def _ev_out_kernel(tl, xc_ref, xs_ref, mc_ref, ms_ref, mod_ref, ng_ref, woc_ref, w1c_ref, w2c_ref,
                   o_ref, wo_ref, w1_ref, w2_ref):
    s = pl.program_id(0)

    @pl.when(s == 0)
    def _zero_pad_rows():
        wo_ref[...] = jnp.zeros_like(wo_ref)

    @pl.when(s < tl.lead)
    def _cast_weights():
        per_head = DV // CROWS
        hh = s // per_head
        dst = jnp.where(s < H * per_head, hh * DVP + (s - hh * per_head) * CROWS,
                        H * DVP + (s - H * per_head) * CROWS)
        wo_ref[pl.ds(pl.multiple_of(dst, CROWS), CROWS), :] = woc_ref[0].astype(BF)
        _cast_chunk(s, w1c_ref, w1_ref)
        _cast_chunk(s, w2c_ref, w2_ref)

    @pl.when(s >= tl.lead)
    def _tile():
        is_ctx = s - tl.lead < tl.n_ctx
        mod = tl.mod_vecs(s, mod_ref)
        x = jnp.where(is_ctx, xc_ref[...], xs_ref[...])
        mix = jnp.where(is_ctx, mc_ref[...], ms_ref[...])
        x1 = x + mod[2] * _dot(mix, wo_ref[...])
        o_ref[...] = _ffn(x1, mod, ng_ref[0], w1_ref, w2_ref)


def _ev_out(tl, xc, xs, mix_c, mix_s, mods, ngf, ev_w_out, ffn_w1, ffn_w2, layer):
    ntok = (tl.n_ctx + tl.n_smp) * TM
    assert DV % CROWS == 0
    return pl.pallas_call(
        functools.partial(_ev_out_kernel, tl),
        grid=(tl.steps,),
        in_specs=tl.specs(D, True) + tl.specs(MIX0, True) + [
            tl.mod_spec(layer), tl.whole(ngf.shape, layer), tl.chunk(0, CROWS, D),
            tl.chunk(layer, CROWS, D_FF), tl.chunk(layer, D_FF // tl.lead, D)],
        out_specs=tl.specs(D, False)[0],
        out_shape=jax.ShapeDtypeStruct((ntok, D), F32),
        scratch_shapes=[pltpu.VMEM((MIX0, D), BF), pltpu.VMEM((D, D_FF), BF),
                        pltpu.VMEM((D_FF, D), BF)],
        compiler_params=pltpu.CompilerParams(
            dimension_semantics=("arbitrary",), vmem_limit_bytes=VMEM_LIMIT),
        name="ev_out_ffn",
    )(xc, xs, mix_c, mix_s, mods, ngf, ev_w_out, ffn_w1, ffn_w2)


def _odd_kernel(tl, ctx_rows, smp_rows, x_ref, mod_ref, ngm_ref, ngf_ref, gfin_ref, ws_ref, gb_ref,
                cw_ref, wic_ref, woc_ref, w1c_ref, w2c_ref, yc_ref, ys_ref,
                wi_ref, wo_ref, w1_ref, w2_ref):
    s = pl.program_id(0)

    @pl.when(s < tl.lead)
    def _cast_weights():
        _cast_chunk(s, wic_ref, wi_ref)
        _cast_chunk(s, woc_ref, wo_ref)
        _cast_chunk(s, w1c_ref, w1_ref)
        _cast_chunk(s, w2c_ref, w2_ref)

    @pl.when(s >= tl.lead)
    def _tile():
        is_ctx = s - tl.lead < tl.n_ctx
        x = x_ref[...]
        mod = tl.mod_vecs(s, mod_ref)
        h = _rms_mod(x, ngm_ref[0], mod[1], mod[0]).astype(BF)
        proj = _dot(h, wi_ref[...])
        u = _gelu_tanh(proj[:, 0:GW])
        v = _gelu_tanh(proj[:, GW:2 * GW]).astype(BF)
        gate_b = proj[:, 2 * GW:2 * GW + CW]
        z = proj[:, 2 * GW + CW:2 * GW + 2 * CW] * proj[:, 2 * GW + 2 * CW:2 * GW + 3 * CW]

        sp_rows = []
        for c in range(TM // GCH):
            cols = []
            for gi in range(GW // GCH):
                vg = v[c * GCH:(c + 1) * GCH, gi * GCH:(gi + 1) * GCH]
                cols.append(_dot(ws_ref[0, gi].astype(BF), vg))
            sp_rows.append(jnp.concatenate(cols, axis=1) + gb_ref[...])
        out_c = u * jnp.concatenate(sp_rows, axis=0)

        last = jnp.where(is_ctx, ctx_rows - 1, smp_rows - 1)
        pos = lax.broadcasted_iota(jnp.int32, (TM, CW), 0) & last
        zl = jnp.where(pos >= 1, pltpu.roll(z, 1, axis=0), 0.0)
        zr = jnp.where(pos < last, pltpu.roll(z, TM - 1, axis=0), 0.0)
        cw = cw_ref[0]
        out_d = gate_b * (zl * cw[0:1] + z * cw[1:2] + zr * cw[2:3])

        mix = jnp.concatenate([out_c, out_d], axis=1).astype(BF)
        x1 = x + mod[2] * _dot(mix, wo_ref[...])
        x2 = _ffn(x1, mod, ngf_ref[0], w1_ref, w2_ref)
        ms = jnp.mean(x2 * x2, axis=-1, keepdims=True)
        y = x2 * lax.rsqrt(ms + EPS) * gfin_ref[...]

        @pl.when(is_ctx)
        def _store_ctx():
            yc_ref[...] = y

        @pl.when(jnp.logical_not(is_ctx))
        def _store_smp():
            ys_ref[...] = y


def _odd(tl, ctx_rows, smp_rows, x, mods, ngm, ngf, gfin, gmlp_ws, gbias, conv_w, od_w_in, od_w_out,
         ffn_w1, ffn_w2, layer):
    for r in (ctx_rows, smp_rows):
        assert TM % r == 0 and r & (r - 1) == 0
    return pl.pallas_call(
        functools.partial(_odd_kernel, tl, ctx_rows, smp_rows),
        grid=(tl.steps,),
        in_specs=tl.specs(D, False) + [
            tl.mod_spec(layer), tl.whole(ngm.shape, layer), tl.whole(ngf.shape, layer),
            tl.whole(gfin.shape), tl.whole(gmlp_ws.shape), tl.whole(gbias.shape),
            tl.whole(conv_w.shape),
            tl.chunk(0, CROWS, OD_COLS), tl.chunk(0, CROWS, D),
            tl.chunk(layer, CROWS, D_FF), tl.chunk(layer, D_FF // tl.lead, D)],
        out_specs=tl.specs(D, True),
        out_shape=[jax.ShapeDtypeStruct((tl.n_ctx * TM, D), F32),
                   jax.ShapeDtypeStruct((tl.n_smp * TM, D), F32)],
        scratch_shapes=[pltpu.VMEM((D, OD_COLS), BF), pltpu.VMEM((D, D), BF),
                        pltpu.VMEM((D, D_FF), BF), pltpu.VMEM((D_FF, D), BF)],
        compiler_params=pltpu.CompilerParams(
            dimension_semantics=("arbitrary",), vmem_limit_bytes=VMEM_LIMIT),
        name="odd_ffn_final",
    )(x, mods, ngm, ngf, gfin, gmlp_ws, gbias, conv_w, od_w_in, od_w_out, ffn_w1, ffn_w2)


def _dft_pos(l):
    n = np.arange(l)
    ang = 2.0 * np.pi * ((n[:, None] * n[None, :]) % l) / l
    to_bf = lambda a: jnp.asarray(a, F32).astype(BF)
    return to_bf(np.cos(ang) / np.sqrt(l)), to_bf(-np.sin(ang) / np.sqrt(l))


def _dft_chan():
    m = np.arange(FG)
    angc = 2.0 * np.pi * ((m[:, None] * m[None, :]) % FG) / FG
    eye = np.eye(FW // FG)
    cc = np.kron(eye, np.cos(angc)) / np.sqrt(FG)
    sc = np.kron(eye, np.sin(angc)) / np.sqrt(FG)
    return jnp.asarray(np.concatenate([cc, sc], axis=1), F32).astype(BF)


def kernel(x_prompt, x_sample, state_gla_fwd, state_gla_bwd, c, c_ctx, ada_w, ada_b, norm_mix_g,
           norm_ffn_g, ffn_w1, ffn_w2, ev_w_in, ev_w_out, gla_w2_f, gla_b2_f, gla_w2_b, gla_b2_b,
           gla_norm_g, od_w_in, od_w_out, gmlp_ws, gmlp_b, conv_w, final_norm_g):
    b_ctx, l_ctx, _ = x_prompt.shape
    b_smp, l_smp, _ = x_sample.shape
    depth = ada_w.shape[0]
    assert l_ctx == TM and l_smp % TM == 0 and b_smp % SUB == 0
    tl = _Tiles(b_ctx * l_ctx // TM, b_smp * l_smp // TM, l_smp // TM, NPREP)

    mods = _modulation(c_ctx, c, ada_w, ada_b)
    ngm = norm_mix_g.reshape(depth, 1, D)
    ngf = norm_ffn_g.reshape(depth, 1, D)
    gbias = jnp.repeat(gmlp_b[0].T, GCH, axis=1)
    xc = x_prompt.reshape(b_ctx * l_ctx, D)
    xs = x_sample.reshape(b_smp * l_smp, D)

    q, k, la, v, g, xcs = _ev_in(tl, xc, xs, mods, ngm, jnp.swapaxes(ev_w_in, 1, 2), gla_w2_f,
                                 gla_w2_b, gla_b2_f, gla_b2_b, _dft_chan())
    mix_c, sf, sb = _gla(b_ctx, l_ctx, 0, q, k, la, v, g, xcs, gla_norm_g, *_dft_pos(l_ctx), None)
    (mix_s,) = _gla(b_smp, l_smp, b_ctx * l_ctx, q, k, la, v, g, xcs, gla_norm_g, *_dft_pos(l_smp),
                    (jnp.swapaxes(state_gla_fwd, -1, -2), jnp.swapaxes(state_gla_bwd, -1, -2)))
    x1 = _ev_out(tl, xc, xs, mix_c, mix_s, mods, ngf, ev_w_out, ffn_w1, ffn_w2, 0)

    yc, ys = _odd(tl, l_ctx, GRID_W, x1, mods, ngm, ngf, final_norm_g.reshape(1, D), gmlp_ws, gbias,
                  conv_w, od_w_in, od_w_out, ffn_w1, ffn_w2, 1)
    return (yc.reshape(b_ctx, l_ctx, D), ys.reshape(b_smp, l_smp, D),
            jnp.swapaxes(sf, -1, -2), jnp.swapaxes(sb, -1, -2))
```

```python
import functools

import jax
import jax.numpy as jnp
import numpy as np
from jax import lax
from jax.experimental import pallas as pl
from jax.experimental.pallas import tpu as pltpu

D = 1024
D_FF = 4 * D
EPS = 1e-6
N_MOD = 6
GRID_W = 64

H = 4
DK = 128
DV = 192
DVP = 256
GLA_LOWRANK = 16
GLA_TAU = 16.0
CH = 64
FW = 256
FG = 64
GW = 512
GCH = 128
CW = 512

EV_V0 = 2 * H * DK
EV_G0 = EV_V0 + H * DV
EV_LR0 = EV_G0 + H * DV
EV_FIN0 = EV_LR0 + 2 * GLA_LOWRANK
EV_IN = EV_FIN0 + FW

TM = 512
FF_SPLIT = 2
LRP = 128
MIX0 = H * DVP + FW
OD_COLS = 2 * GW + 3 * CW
GLA_UNROLL = 8
SUB = 8
LANE = 128

VMEM_LIMIT = 60 * 1024 * 1024

BF = jnp.bfloat16
F32 = jnp.float32


def _dot(a, b):
    return jnp.dot(a, b, preferred_element_type=F32)


def _dot_nt(a, b):
    return lax.dot_general(a, b, (((1,), (1,)), ((), ())), preferred_element_type=F32)


def _dot_tn(a, b):
    return lax.dot_general(a, b, (((0,), (0,)), ((), ())), preferred_element_type=F32)


def _sigmoid(x):
    return 1.0 / (1.0 + jnp.exp(-x))


def _silu(x):
    return x * _sigmoid(x)


def _log_sigmoid(x):
    return jnp.minimum(x, 0.0) - jnp.log(1.0 + jnp.exp(-jnp.abs(x)))


def _gelu_tanh(x):
    c = np.float32(np.sqrt(2.0 / np.pi))
    return 0.5 * x * (1.0 + jnp.tanh(c * (x + 0.044715 * (x * x * x))))


def _rms_mod(x, g, sc, sh):
    ms = jnp.mean(x * x, axis=-1, keepdims=True)
    return (x * lax.rsqrt(ms + EPS)) * (g * (1.0 + sc)) + sh


def _pair_heads(f, b):
    return jnp.concatenate([p for hh in range(H)
                            for p in (f[:, hh * DK:(hh + 1) * DK], b[:, hh * DK:(hh + 1) * DK])], axis=1)


class _Tiles:
    def __init__(self, n_ctx, n_smp, tiles_per_smp_seq, lead=0):
        self.n_ctx, self.n_smp, self.tps, self.lead = n_ctx, n_smp, tiles_per_smp_seq, lead
        self.steps = lead + n_ctx + n_smp

    def tile(self, s):
        return jnp.maximum(s - self.lead, 0)

    def ctx(self, s):
        return jnp.minimum(self.tile(s), self.n_ctx - 1)

    def smp(self, s):
        return jnp.maximum(self.tile(s) - self.n_ctx, 0)

    def cond_row(self, s):
        return jnp.where(self.tile(s) < self.n_ctx, 0, SUB + self.smp(s) // self.tps)

    def specs(self, width, dual):
        if dual:
            return [pl.BlockSpec((TM, width), lambda s: (self.ctx(s), 0)),
                    pl.BlockSpec((TM, width), lambda s: (self.smp(s), 0))]
        return [pl.BlockSpec((TM, width), lambda s: (self.tile(s), 0))]

    def mod_spec(self, layer):
        return pl.BlockSpec((1, SUB, N_MOD * D), lambda s: (layer, self.cond_row(s) // SUB, 0))

    def mod_vecs(self, s, mod_ref):
        row = mod_ref[0, pl.ds(self.cond_row(s) % SUB, 1), :]
        return [row[:, i * D:(i + 1) * D] for i in range(N_MOD)]

    def whole(self, shape, layer=None):
        if layer is None:
            return pl.BlockSpec(shape, lambda s: tuple(0 for _ in shape))
        return pl.BlockSpec((1,) + shape[1:], lambda s: (layer,) + tuple(0 for _ in shape[1:]))

    def weight(self, shape, layer=None):
        if layer is None:
            return pl.BlockSpec(shape, lambda s: (0, 0), pipeline_mode=pl.Buffered(1))
        return pl.BlockSpec((1,) + shape[1:], lambda s: (layer, 0, 0), pipeline_mode=pl.Buffered(1))


def _mod_kernel(cx_ref, c_ref, w_ref, b_ref, o_ref):
    w = w_ref[0].astype(BF)
    ax = jnp.broadcast_to(_silu(cx_ref[...]), (SUB, D)).astype(BF)
    o_ref[0, 0:SUB, :] = _dot(ax, w) + b_ref[0]
    o_ref[0, SUB:, :] = _dot(_silu(c_ref[...]).astype(BF), w) + b_ref[0]


def _modulation(c_ctx, c, ada_w, ada_b):
    depth = ada_w.shape[0]
    r = SUB + c.shape[0]
    tn = 1024
    return pl.pallas_call(
        _mod_kernel,
        grid=(depth, N_MOD * D // tn),
        in_specs=[
            pl.BlockSpec((1, D), lambda l, j: (0, 0)),
            pl.BlockSpec(c.shape, lambda l, j: (0, 0)),
            pl.BlockSpec((1, D, tn), lambda l, j: (l, 0, j)),
            pl.BlockSpec((1, 1, tn), lambda l, j: (l, 0, j)),
        ],
        out_specs=pl.BlockSpec((1, r, tn), lambda l, j: (l, 0, j)),
        out_shape=jax.ShapeDtypeStruct((depth, r, N_MOD * D), F32),
        compiler_params=pltpu.CompilerParams(
            dimension_semantics=("arbitrary", "arbitrary"), vmem_limit_bytes=VMEM_LIMIT),
        name="modulation",
    )(c_ctx.reshape(1, D), c, ada_w, ada_b.reshape(depth, 1, N_MOD * D))


def _ev_in_kernel(tl, xc_ref, xs_ref, mod_ref, ng_ref, wc_ref, w2f_ref, w2b_ref, b2f_ref, b2b_ref,
                  dft_ref, q_ref, k_ref, la_ref, v_ref, g_ref, xcs_ref,
                  wqk_ref, wvg_ref, wlr_ref, wfin_ref, w2_ref):
    s = pl.program_id(0)
    c0 = H * DK

    @pl.when(s == 0)
    def _gate_weights():
        z = jnp.zeros((GLA_LOWRANK, c0), F32)
        w2_ref[...] = jnp.zeros_like(w2_ref)
        w2_ref[0:GLA_LOWRANK, :] = _pair_heads(w2f_ref[0], z).astype(BF)
        w2_ref[GLA_LOWRANK:2 * GLA_LOWRANK, :] = _pair_heads(z, w2b_ref[0]).astype(BF)

    @pl.when(s < tl.lead)
    def _cast_weights():
        w = wc_ref[0]
        r0 = pl.multiple_of(s * LANE, LANE)
        rows = pl.ds(r0, LANE)

        def tr(lo, hi, pad):
            piece = w[lo:hi, :]
            if pad:
                piece = jnp.concatenate([piece, jnp.zeros((pad, LANE), F32)], axis=0)
            return piece.T.astype(BF)

        wqk_ref[rows, :] = tr(0, EV_V0, 0)
        for hh in range(H):
            wvg_ref[rows, hh * DVP:(hh + 1) * DVP] = tr(EV_V0 + hh * DV, EV_V0 + (hh + 1) * DV, DVP - DV)
            wvg_ref[rows, (H + hh) * DVP:(H + hh + 1) * DVP] = tr(
                EV_G0 + hh * DV, EV_G0 + (hh + 1) * DV, DVP - DV)
        wlr_ref[rows, :] = tr(EV_LR0, EV_FIN0, LRP - 2 * GLA_LOWRANK)
        wfin_ref[rows, :] = tr(EV_FIN0, EV_IN, 0)

    @pl.when(s >= tl.lead)
    def _tile():
        x = jnp.where(s - tl.lead < tl.n_ctx, xc_ref[...], xs_ref[...])
        mod = tl.mod_vecs(s, mod_ref)
        h = _rms_mod(x, ng_ref[0], mod[1], mod[0]).astype(BF)

        lr = _dot(h, wlr_ref[...]).astype(BF)
        pre = _dot(lr, w2_ref[...]) + _pair_heads(b2f_ref[...], b2b_ref[...])
        la_ref[...] = _log_sigmoid(pre) * np.float32(1.0 / GLA_TAU)

        qk = _dot(h, wqk_ref[...])
        q_ref[...] = (qk[:, 0:c0] * np.float32(DK ** -0.5)).astype(BF)
        k_ref[...] = qk[:, c0:2 * c0].astype(BF)
        vg = _dot(h, wvg_ref[...])
        v_ref[...] = vg[:, 0:H * DVP].astype(BF)
        g_ref[...] = vg[:, H * DVP:2 * H * DVP].astype(BF)
        fin = _dot(h, wfin_ref[...]).astype(BF)
        xcs_ref[...] = _dot(fin, dft_ref[...]).astype(BF)


def _ev_in(tl, xc, xs, mods, ngm, ev_w_in_t, w2f, w2b, b2f, b2b, dftc):
    c0 = H * DK
    assert tl.lead == D // LANE
    ntok = (tl.n_ctx + tl.n_smp) * TM
    widths = [(c0, BF), (c0, BF), (2 * c0, F32), (H * DVP, BF), (H * DVP, BF), (2 * FW, BF)]
    return pl.pallas_call(
        functools.partial(_ev_in_kernel, tl),
        grid=(tl.steps,),
        in_specs=tl.specs(D, True) + [
            tl.mod_spec(0), tl.whole(ngm.shape, 0),
            pl.BlockSpec((1, EV_IN, LANE), lambda s: (0, 0, jnp.minimum(s, tl.lead - 1))),
            tl.whole(w2f.shape), tl.whole(w2b.shape), tl.whole(b2f.shape), tl.whole(b2b.shape),
            tl.whole(dftc.shape)],
        out_specs=[tl.specs(w, False)[0] for w, _ in widths],
        out_shape=[jax.ShapeDtypeStruct((ntok, w), dt) for w, dt in widths],
        scratch_shapes=[pltpu.VMEM((D, 2 * c0), BF), pltpu.VMEM((D, 2 * H * DVP), BF),
                        pltpu.VMEM((D, LRP), BF), pltpu.VMEM((D, FW), BF),
                        pltpu.VMEM((LRP, 2 * c0), BF)],
        compiler_params=pltpu.CompilerParams(
            dimension_semantics=("arbitrary",), vmem_limit_bytes=VMEM_LIMIT),
        name="ev_in",
    )(xc, xs, mods, ngm, ev_w_in_t, w2f, w2b, b2f, b2b, dftc)


def _chunk_scan(x, pos, reverse):
    s = 1
    while s < CH:
        if reverse:
            y = pltpu.roll(x, CH - s, axis=0)
            x = x + jnp.where(pos < CH - s, y, 0.0)
        else:
            y = pltpu.roll(x, s, axis=0)
            x = x + jnp.where(pos >= s, y, 0.0)
        s *= 2
    return x


def _gla_kernel(l, has_state, n_cast, pad_wo, *refs):
    q_ref, k_ref, la_ref, v_ref, g_ref, xcs_ref, gn_ref, cl_ref, sl_ref = refs[:9]
    refs = refs[9:]
    if has_state:
        s0f_ref, s0b_ref = refs[:2]
        refs = refs[2:]
    cast_src, refs = refs[:n_cast], refs[n_cast:]
    mix_ref, refs = refs[0], refs[1:]
    if not has_state:
        sf_ref, sb_ref = refs[:2]
        refs = refs[2:]
    cast_dst, refs = refs[:n_cast], refs[n_cast:]
    bc_ref, ke_ref, dec_ref, u_ref, sp_ref, st_ref = refs

    step = pl.program_id(0)
    for j, (src, dst) in enumerate(zip(cast_src, cast_dst)):
        if pad_wo and j == 0:
            rows = src.shape[1]
            per_head = DV // rows

            @pl.when(step == 0)
            def _zero_pad_rows(dst=dst):
                dst[...] = jnp.zeros_like(dst)

            hd = step // per_head
            r0 = jnp.where(step < H * per_head, hd * DVP + (step - hd * per_head) * rows,
                           H * DVP + (step - H * per_head) * rows)
            dst[pl.ds(pl.multiple_of(r0, rows), rows), :] = src[0].astype(BF)
        else:
            dst[...] = src[0].astype(BF)

    nchunks = l // CH
    unroll = min(GLA_UNROLL, nchunks)
    pos = lax.broadcasted_iota(jnp.int32, (CH, DK), 0)
    row = lax.broadcasted_iota(jnp.int32, (CH, CH), 0)
    col = lax.broadcasted_iota(jnp.int32, (CH, CH), 1)
    gn = jnp.concatenate([gn_ref[...], jnp.zeros((1, DVP - DV), F32)], axis=1)

    for hh in range(H):
        ks = slice(hh * DK, (hh + 1) * DK)
        vs = slice(hh * DVP, (hh + 1) * DVP)

        def inc_body(n, carry):
            r0 = pl.multiple_of(n * CH, CH)
            la = la_ref[pl.ds(r0, CH), vs]
            bf = _chunk_scan(la[:, 0:DK], pos, False)
            bb = _chunk_scan(la[:, DK:2 * DK], pos, True)
            bc_ref[pl.ds(r0, CH), 0:DK] = bf
            bc_ref[pl.ds(r0, CH), DK:2 * DK] = bb
            dec_ref[n, :, 0:DK] = jnp.exp(bf[CH - 1:CH, :])
            dec_ref[n, :, DK:2 * DK] = jnp.exp(bb[0:1, :])
            kk = k_ref[pl.ds(r0, CH), ks].astype(F32)
            ke = jnp.concatenate([kk * jnp.exp(-bf), kk * jnp.exp(-bb)], axis=1).astype(BF)
            ke_ref[pl.ds(r0, CH), :] = ke
            u_ref[n] = _dot_tn(v_ref[pl.ds(r0, CH), vs], ke)
            return carry

        lax.fori_loop(0, nchunks, inc_body, 0, unroll=unroll)

        if has_state:
            st_ref[0:DV, 0:DK] = s0f_ref[0, 0, hh]
            st_ref[0:DV, DK:2 * DK] = s0b_ref[0, 0, hh]
            st_ref[DV:DVP, :] = jnp.zeros((DVP - DV, 2 * DK), F32)
        else:
            st_ref[...] = jnp.zeros_like(st_ref)

        def scan_body(i, carry):
            nf = i
            nb = nchunks - 1 - i
            sf = st_ref[:, 0:DK]
            sb = st_ref[:, DK:2 * DK]
            sp_ref[nf, :, 0:DK] = sf.astype(BF)
            sp_ref[nb, :, DK:2 * DK] = sb.astype(BF)
            st_ref[:, 0:DK] = dec_ref[nf, :, 0:DK] * (sf + u_ref[nf, :, 0:DK])
            st_ref[:, DK:2 * DK] = dec_ref[nb, :, DK:2 * DK] * (sb + u_ref[nb, :, DK:2 * DK])
            return carry

        lax.fori_loop(0, nchunks, scan_body, 0)
        if not has_state:
            sf_ref[0, 0, hh] = st_ref[0:DV, 0:DK]
            sb_ref[0, 0, hh] = st_ref[0:DV, DK:2 * DK]

        def out_body(n, carry):
            r0 = pl.multiple_of(n * CH, CH)
            bc = bc_ref[pl.ds(r0, CH), :]
            qq = q_ref[pl.ds(r0, CH), ks].astype(F32)
            qe = jnp.concatenate([qq * jnp.exp(bc[:, 0:DK]), qq * jnp.exp(bc[:, DK:2 * DK])],
                                 axis=1).astype(BF)
            ke = ke_ref[pl.ds(r0, CH), :]
            att = (jnp.where(row >= col, _dot_nt(qe[:, 0:DK], ke[:, 0:DK]), 0.0)
                   + jnp.where(row <= col, _dot_nt(qe[:, DK:], ke[:, DK:]), 0.0)).astype(BF)
            o = _dot(att, v_ref[pl.ds(r0, CH), vs]) + _dot_nt(qe, sp_ref[n])
            ms = jnp.sum(o * o, axis=-1, keepdims=True) * np.float32(1.0 / DV)
            on = o * lax.rsqrt(ms + EPS) * gn
            gate = _silu(g_ref[pl.ds(r0, CH), vs].astype(F32))
            mix_ref[pl.ds(r0, CH), vs] = (on * gate).astype(BF)
            return carry

        lax.fori_loop(0, nchunks, out_body, 0, unroll=unroll)

    fo = _dot(cl_ref[...], xcs_ref[:, 0:FW]) + _dot(sl_ref[...], xcs_ref[:, FW:2 * FW])
    mix_ref[:, H * DVP:H * DVP + FW] = fo.astype(BF)


def _gla(b, l, tok0, q, k, la, v, g, xcs, gn, cl, sl, states, casts, pad_wo=False):
    nchunks = l // CH
    c0 = H * DK
    assert tok0 % l == 0
    seq = lambda w: pl.BlockSpec((l, w), lambda i: (tok0 // l + i, 0))
    const = lambda shape: pl.BlockSpec(shape, lambda i: tuple(0 for _ in shape))
    once = lambda shape: pl.BlockSpec(shape, lambda i: (0, 0), pipeline_mode=pl.Buffered(1))
    st_spec = pl.BlockSpec((1, 1, H, DV, DK), lambda i: (i, 0, 0, 0, 0))
    in_specs = [seq(c0), seq(c0), seq(2 * c0), seq(H * DVP), seq(H * DVP), seq(2 * FW),
                const(gn.shape), once((l, l)), once((l, l))]
    out_specs = [pl.BlockSpec((l, MIX0), lambda i: (i, 0))]
    out_shape = [jax.ShapeDtypeStruct((b * l, MIX0), BF)]
    args = [q, k, la, v, g, xcs, gn, cl, sl]
    if states is not None:
        in_specs += [st_spec, st_spec]
        args += list(states)
    else:
        out_specs += [st_spec, st_spec]
        out_shape += [jax.ShapeDtypeStruct((b, 1, H, DV, DK), F32)] * 2
    for j, (w, layer) in enumerate(casts):
        _, r, c = w.shape
        assert r % b == 0
        in_specs.append(pl.BlockSpec((1, r // b, c), lambda i, layer=layer: (layer, i, 0)))
        args.append(w)
        if pad_wo and j == 0:
            assert r == H * DV + FW and DV % (r // b) == 0
            out_specs.append(const((MIX0, c)))
            out_shape.append(jax.ShapeDtypeStruct((MIX0, c), BF))
        else:
            out_specs.append(pl.BlockSpec((r // b, c), lambda i: (i, 0)))
            out_shape.append(jax.ShapeDtypeStruct((r, c), BF))
    return pl.pallas_call(
        functools.partial(_gla_kernel, l, states is not None, len(casts), pad_wo),
        grid=(b,),
        in_specs=in_specs,
        out_specs=out_specs,
        out_shape=out_shape,
        scratch_shapes=[pltpu.VMEM((l, 2 * DK), F32),
                        pltpu.VMEM((l, 2 * DK), BF),
                        pltpu.VMEM((nchunks, 1, 2 * DK), F32),
                        pltpu.VMEM((nchunks, DVP, 2 * DK), F32),
                        pltpu.VMEM((nchunks, DVP, 2 * DK), BF),
                        pltpu.VMEM((DVP, 2 * DK), F32)],
        compiler_params=pltpu.CompilerParams(
            dimension_semantics=("arbitrary",), vmem_limit_bytes=VMEM_LIMIT),
        name="gla_fnet",
    )(*args)


def _ffn(x1, mod, ng, w1_ref, w2_ref):
    h = _rms_mod(x1, ng, mod[4], mod[3]).astype(BF)
    wd = D_FF // FF_SPLIT
    out = None
    for c in range(FF_SPLIT):
        hid = jnp.maximum(_dot(h, w1_ref[:, c * wd:(c + 1) * wd]), 0.0)
        part = _dot((hid * hid).astype(BF), w2_ref[c * wd:(c + 1) * wd, :])
        out = part if out is None else out + part
    return x1 + mod[5] * out


def _ev_out_kernel(tl, xc_ref, xs_ref, mc_ref, ms_ref, mod_ref, ng_ref, wo_ref, w1_ref, w2_ref,
                   o_ref):
    s = pl.program_id(0)
    is_ctx = s < tl.n_ctx
    mod = tl.mod_vecs(s, mod_ref)
    x = jnp.where(is_ctx, xc_ref[...], xs_ref[...])
    mix = jnp.where(is_ctx, mc_ref[...], ms_ref[...])
    x1 = x + mod[2] * _dot(mix, wo_ref[...])
    o_ref[...] = _ffn(x1, mod, ng_ref[0], w1_ref, w2_ref)


def _ev_out(tl, xc, xs, mix_c, mix_s, mods, ngf, wo, w1, w2, layer):
    ntok = (tl.n_ctx + tl.n_smp) * TM
    return pl.pallas_call(
        functools.partial(_ev_out_kernel, tl),
        grid=(tl.steps,),
        in_specs=tl.specs(D, True) + tl.specs(MIX0, True) + [
            tl.mod_spec(layer), tl.whole(ngf.shape, layer),
            tl.weight(wo.shape), tl.weight(w1.shape), tl.weight(w2.shape)],
        out_specs=tl.specs(D, False)[0],
        out_shape=jax.ShapeDtypeStruct((ntok, D), F32),
        compiler_params=pltpu.CompilerParams(
            dimension_semantics=("arbitrary",), vmem_limit_bytes=VMEM_LIMIT),
        name="ev_out_ffn",
    )(xc, xs, mix_c, mix_s, mods, ngf, wo, w1, w2)


def _odd_kernel(tl, ctx_rows, smp_rows, x_ref, mod_ref, ngm_ref, ngf_ref, gfin_ref, ws_ref, gb_ref,
                cw_ref, wi_ref, wo_ref, w1_ref, w2_ref, yc_ref, ys_ref):
    s = pl.program_id(0)
    is_ctx = s < tl.n_ctx
    x = x_ref[...]
    mod = tl.mod_vecs(s, mod_ref)
    h = _rms_mod(x, ngm_ref[0], mod[1], mod[0]).astype(BF)
    proj = _dot(h, wi_ref[...])
    u = _gelu_tanh(proj[:, 0:GW])
    v = _gelu_tanh(proj[:, GW:2 * GW]).astype(BF)
    gate_b = proj[:, 2 * GW:2 * GW + CW]
    z = proj[:, 2 * GW + CW:2 * GW + 2 * CW] * proj[:, 2 * GW + 2 * CW:2 * GW + 3 * CW]

    sp_rows = []
    for c in range(TM // GCH):
        cols = []
        for gi in range(GW // GCH):
            vg = v[c * GCH:(c + 1) * GCH, gi * GCH:(gi + 1) * GCH]
            cols.append(_dot(ws_ref[0, gi].astype(BF), vg))
        sp_rows.append(jnp.concatenate(cols, axis=1) + gb_ref[...])
    out_c = u * jnp.concatenate(sp_rows, axis=0)

    last = jnp.where(is_ctx, ctx_rows - 1, smp_rows - 1)
    pos = lax.broadcasted_iota(jnp.int32, (TM, CW), 0) & last
    zl = jnp.where(pos >= 1, pltpu.roll(z, 1, axis=0), 0.0)
    zr = jnp.where(pos < last, pltpu.roll(z, TM - 1, axis=0), 0.0)
    cw = cw_ref[0]
    out_d = gate_b * (zl * cw[0:1] + z * cw[1:2] + zr * cw[2:3])

    mix = jnp.concatenate([out_c, out_d], axis=1).astype(BF)
    x1 = x + mod[2] * _dot(mix, wo_ref[...])
    x2 = _ffn(x1, mod, ngf_ref[0], w1_ref, w2_ref)
    ms = jnp.mean(x2 * x2, axis=-1, keepdims=True)
    y = x2 * lax.rsqrt(ms + EPS) * gfin_ref[...]

    @pl.when(is_ctx)
    def _store_ctx():
        yc_ref[...] = y

    @pl.when(jnp.logical_not(is_ctx))
    def _store_smp():
        ys_ref[...] = y


def _odd(tl, ctx_rows, smp_rows, x, mods, ngm, ngf, gfin, gmlp_ws, gbias, conv_w, wi, wo, w1, w2,
         layer):
    for r in (ctx_rows, smp_rows):
        assert TM % r == 0 and r & (r - 1) == 0
    assert TM % GCH == 0
    return pl.pallas_call(
        functools.partial(_odd_kernel, tl, ctx_rows, smp_rows),
        grid=(tl.steps,),
        in_specs=tl.specs(D, False) + [
            tl.mod_spec(layer), tl.whole(ngm.shape, layer), tl.whole(ngf.shape, layer),
            tl.whole(gfin.shape), tl.whole(gmlp_ws.shape), tl.whole(gbias.shape),
            tl.whole(conv_w.shape),
            tl.weight(wi.shape), tl.weight(wo.shape), tl.weight(w1.shape), tl.weight(w2.shape)],
        out_specs=tl.specs(D, True),
        out_shape=[jax.ShapeDtypeStruct((tl.n_ctx * TM, D), F32),
                   jax.ShapeDtypeStruct((tl.n_smp * TM, D), F32)],
        compiler_params=pltpu.CompilerParams(
            dimension_semantics=("arbitrary",), vmem_limit_bytes=VMEM_LIMIT),
        name="odd_ffn_final",
    )(x, mods, ngm, ngf, gfin, gmlp_ws, gbias, conv_w, wi, wo, w1, w2)


def _dft_pos(l):
    n = np.arange(l)
    ang = 2.0 * np.pi * ((n[:, None] * n[None, :]) % l) / l
    to_bf = lambda a: jnp.asarray(a, F32).astype(BF)
    return to_bf(np.cos(ang) / np.sqrt(l)), to_bf(-np.sin(ang) / np.sqrt(l))


def _dft_chan():
    m = np.arange(FG)
    angc = 2.0 * np.pi * ((m[:, None] * m[None, :]) % FG) / FG
    eye = np.eye(FW // FG)
    cc = np.kron(eye, np.cos(angc)) / np.sqrt(FG)
    sc = np.kron(eye, np.sin(angc)) / np.sqrt(FG)
    return jnp.asarray(np.concatenate([cc, sc], axis=1), F32).astype(BF)


def kernel(x_prompt, x_sample, state_gla_fwd, state_gla_bwd, c, c_ctx, ada_w, ada_b, norm_mix_g,
           norm_ffn_g, ffn_w1, ffn_w2, ev_w_in, ev_w_out, gla_w2_f, gla_b2_f, gla_w2_b, gla_b2_b,
           gla_norm_g, od_w_in, od_w_out, gmlp_ws, gmlp_b, conv_w, final_norm_g):
    b_ctx, l_ctx, _ = x_prompt.shape
    b_smp, l_smp, _ = x_sample.shape
    depth = ada_w.shape[0]
    assert TM % l_ctx == 0 and l_smp % TM == 0 and b_smp % SUB == 0
    n_ctx, n_smp, tps = b_ctx * l_ctx // TM, b_smp * l_smp // TM, l_smp // TM
    tl = _Tiles(n_ctx, n_smp, tps)

    mods = _modulation(c_ctx, c, ada_w, ada_b)
    ngm = norm_mix_g.reshape(depth, 1, D)
    ngf = norm_ffn_g.reshape(depth, 1, D)
    gbias = jnp.repeat(gmlp_b[0].T, GCH, axis=1)
    xc = x_prompt.reshape(b_ctx * l_ctx, D)
    xs = x_sample.reshape(b_smp * l_smp, D)

    q, k, la, v, g, xcs = _ev_in(_Tiles(n_ctx, n_smp, tps, lead=D // LANE), xc, xs, mods, ngm,
                                 jnp.swapaxes(ev_w_in, 1, 2), gla_w2_f, gla_w2_b, gla_b2_f,
                                 gla_b2_b, _dft_chan())
    mix_c, sf, sb, wo0, w1_0, w2_0 = _gla(
        b_ctx, l_ctx, 0, q, k, la, v, g, xcs, gla_norm_g, *_dft_pos(l_ctx), None,
        [(ev_w_out, 0), (ffn_w1, 0), (ffn_w2, 0)], pad_wo=True)
    mix_s, wi1, wo1, w1_1, w2_1 = _gla(
        b_smp, l_smp, b_ctx * l_ctx, q, k, la, v, g, xcs, gla_norm_g, *_dft_pos(l_smp),
        (jnp.swapaxes(state_gla_fwd, -1, -2), jnp.swapaxes(state_gla_bwd, -1, -2)),
        [(od_w_in, 0), (od_w_out, 0), (ffn_w1, 1), (ffn_w2, 1)])
    x1 = _ev_out(tl, xc, xs, mix_c, mix_s, mods, ngf, wo0, w1_0, w2_0, 0)

    yc, ys = _odd(tl, l_ctx, GRID_W, x1, mods, ngm, ngf, final_norm_g.reshape(1, D), gmlp_ws, gbias,
                  conv_w, wi1, wo1, w1_1, w2_1, 1)
    return (yc.reshape(b_ctx, l_ctx, D), ys.reshape(b_smp, l_smp, D),
            jnp.swapaxes(sf, -1, -2), jnp.swapaxes(sb, -1, -2))
```

```python
import functools

import jax
import jax.numpy as jnp
import numpy as np
from jax import lax
from jax.experimental import pallas as pl
from jax.experimental.pallas import tpu as pltpu

D = 1024
D_FF = 4 * D
EPS = 1e-6
N_MOD = 6
GRID_W = 64

H = 4
DK = 128
DV = 192
DVP = 256
GLA_LOWRANK = 16
GLA_TAU = 16.0
CH = 64
FW = 256
FG = 64
GW = 512
GCH = 128
CW = 512

EV_V0 = 2 * H * DK
EV_G0 = EV_V0 + H * DV
EV_LR0 = EV_G0 + H * DV
EV_FIN0 = EV_LR0 + 2 * GLA_LOWRANK
EV_IN = EV_FIN0 + FW

TM = 512
FF_SPLIT = 2
LRP = 128
MIX0 = H * DVP + FW
OD_COLS = 2 * GW + 3 * CW
GLA_UNROLL = 16
SUB = 8
LANE = 128

VMEM_LIMIT = 60 * 1024 * 1024

BF = jnp.bfloat16
F32 = jnp.float32


def _dot(a, b):
    return jnp.dot(a, b, preferred_element_type=F32)


def _dot_nt(a, b):
    return lax.dot_general(a, b, (((1,), (1,)), ((), ())), preferred_element_type=F32)


def _dot_tn(a, b):
    return lax.dot_general(a, b, (((0,), (0,)), ((), ())), preferred_element_type=F32)


def _sigmoid(x):
    return 1.0 / (1.0 + jnp.exp(-x))


def _silu(x):
    return x * _sigmoid(x)


def _log_sigmoid(x):
    return jnp.minimum(x, 0.0) - jnp.log(1.0 + jnp.exp(-jnp.abs(x)))


def _gelu_tanh(x):
    c = np.float32(np.sqrt(2.0 / np.pi))
    return 0.5 * x * (1.0 + jnp.tanh(c * (x + 0.044715 * (x * x * x))))


def _rms_mod(x, g, sc, sh):
    ms = jnp.mean(x * x, axis=-1, keepdims=True)
    return (x * lax.rsqrt(ms + EPS)) * (g * (1.0 + sc)) + sh


def _pair_heads(f, b):
    return jnp.concatenate([p for hh in range(H)
                            for p in (f[:, hh * DK:(hh + 1) * DK], b[:, hh * DK:(hh + 1) * DK])], axis=1)


class _Tiles:
    def __init__(self, n_ctx, n_smp, tiles_per_smp_seq, lead=0):
        self.n_ctx, self.n_smp, self.tps, self.lead = n_ctx, n_smp, tiles_per_smp_seq, lead
        self.steps = lead + n_ctx + n_smp

    def tile(self, s):
        return jnp.maximum(s - self.lead, 0)

    def ctx(self, s):
        return jnp.minimum(self.tile(s), self.n_ctx - 1)

    def smp(self, s):
        return jnp.maximum(self.tile(s) - self.n_ctx, 0)

    def cond_row(self, s):
        return jnp.where(self.tile(s) < self.n_ctx, 0, SUB + self.smp(s) // self.tps)

    def specs(self, width, dual):
        if dual:
            return [pl.BlockSpec((TM, width), lambda s: (self.ctx(s), 0)),
                    pl.BlockSpec((TM, width), lambda s: (self.smp(s), 0))]
        return [pl.BlockSpec((TM, width), lambda s: (self.tile(s), 0))]

    def mod_spec(self):
        return pl.BlockSpec((1, SUB, N_MOD * D), lambda s: (0, self.cond_row(s) // SUB, 0))

    def mod_vecs(self, s, mod_ref):
        row = mod_ref[0, pl.ds(self.cond_row(s) % SUB, 1), :]
        return [row[:, i * D:(i + 1) * D] for i in range(N_MOD)]

    def whole(self, shape, layer=None):
        if layer is None:
            return pl.BlockSpec(shape, lambda s: tuple(0 for _ in shape))
        return pl.BlockSpec((1,) + shape[1:], lambda s: (layer,) + tuple(0 for _ in shape[1:]))

    def weight(self, shape, layer=None):
        if layer is None:
            return pl.BlockSpec(shape, lambda s: (0, 0), pipeline_mode=pl.Buffered(1))
        return pl.BlockSpec((1,) + shape[1:], lambda s: (layer, 0, 0), pipeline_mode=pl.Buffered(1))


def _mod_kernel(cx_ref, c_ref, w_ref, b_ref, o_ref):
    w = w_ref[0].astype(BF)
    ax = jnp.broadcast_to(_silu(cx_ref[...]), (SUB, D)).astype(BF)
    o_ref[0, 0:SUB, :] = _dot(ax, w) + b_ref[0]
    o_ref[0, SUB:, :] = _dot(_silu(c_ref[...]).astype(BF), w) + b_ref[0]


def _mod_specs(c, layer, steps):
    tn = N_MOD * D // steps
    assert tn % LANE == 0
    return ([pl.BlockSpec((1, D), lambda j: (0, 0)),
             pl.BlockSpec(c.shape, lambda j: (0, 0)),
             pl.BlockSpec((1, D, tn), lambda j: (layer, 0, j)),
             pl.BlockSpec((1, 1, tn), lambda j: (layer, 0, j))],
            pl.BlockSpec((1, SUB + c.shape[0], tn), lambda j: (0, 0, j)),
            jax.ShapeDtypeStruct((1, SUB + c.shape[0], N_MOD * D), F32))


def _modulation(c_ctx, c, ada_w, ada_b, layer):
    steps = N_MOD
    in_specs, out_spec, out_shape = _mod_specs(c, layer, steps)
    return pl.pallas_call(
        _mod_kernel,
        grid=(steps,),
        in_specs=in_specs,
        out_specs=out_spec,
        out_shape=out_shape,
        compiler_params=pltpu.CompilerParams(
            dimension_semantics=("arbitrary",), vmem_limit_bytes=VMEM_LIMIT),
        name="modulation",
    )(c_ctx, c, ada_w, ada_b)


def _ev_in_kernel(tl, xc_ref, xs_ref, mod_ref, ng_ref, wc_ref, w2f_ref, w2b_ref, b2f_ref, b2b_ref,
                  dft_ref, q_ref, k_ref, la_ref, v_ref, g_ref, xcs_ref,
                  wqk_ref, wvg_ref, wlr_ref, wfin_ref, w2_ref):
    s = pl.program_id(0)
    c0 = H * DK

    @pl.when(s == 0)
    def _gate_weights():
        z = jnp.zeros((GLA_LOWRANK, c0), F32)
        w2_ref[...] = jnp.zeros_like(w2_ref)
        w2_ref[0:GLA_LOWRANK, :] = _pair_heads(w2f_ref[0], z).astype(BF)
        w2_ref[GLA_LOWRANK:2 * GLA_LOWRANK, :] = _pair_heads(z, w2b_ref[0]).astype(BF)

    @pl.when(s < tl.lead)
    def _cast_weights():
        w = wc_ref[0]
        r0 = pl.multiple_of(s * LANE, LANE)
        rows = pl.ds(r0, LANE)

        def tr(lo, hi, pad):
            piece = w[lo:hi, :]
            if pad:
                piece = jnp.concatenate([piece, jnp.zeros((pad, LANE), F32)], axis=0)
            return piece.T.astype(BF)

        wqk_ref[rows, :] = tr(0, EV_V0, 0)
        for hh in range(H):
            wvg_ref[rows, hh * DVP:(hh + 1) * DVP] = tr(EV_V0 + hh * DV, EV_V0 + (hh + 1) * DV, DVP - DV)
            wvg_ref[rows, (H + hh) * DVP:(H + hh + 1) * DVP] = tr(
                EV_G0 + hh * DV, EV_G0 + (hh + 1) * DV, DVP - DV)
        wlr_ref[rows, :] = tr(EV_LR0, EV_FIN0, LRP - 2 * GLA_LOWRANK)
        wfin_ref[rows, :] = tr(EV_FIN0, EV_IN, 0)

    @pl.when(s >= tl.lead)
    def _tile():
        x = jnp.where(s - tl.lead < tl.n_ctx, xc_ref[...], xs_ref[...])
        mod = tl.mod_vecs(s, mod_ref)
        h = _rms_mod(x, ng_ref[0], mod[1], mod[0]).astype(BF)

        lr = _dot(h, wlr_ref[...]).astype(BF)
        b2 = _pair_heads(b2f_ref[...], b2b_ref[...])
        pw = 2 * DK
        for hh in range(H):
            cs = slice(hh * pw, (hh + 1) * pw)
            pre = _dot(lr, w2_ref[:, cs]) + b2[:, cs]
            la_ref[:, cs] = _log_sigmoid(pre) * np.float32(1.0 / GLA_TAU)
            vs = slice(hh * DVP, (hh + 1) * DVP)
            v_ref[:, vs] = _dot(h, wvg_ref[:, vs]).astype(BF)
            gs = slice((H + hh) * DVP, (H + hh + 1) * DVP)
            g_ref[:, vs] = _dot(h, wvg_ref[:, gs]).astype(BF)
        q_ref[...] = (_dot(h, wqk_ref[:, 0:c0]) * np.float32(DK ** -0.5)).astype(BF)
        k_ref[...] = _dot(h, wqk_ref[:, c0:2 * c0]).astype(BF)
        fin = _dot(h, wfin_ref[...]).astype(BF)
        xcs_ref[...] = _dot(fin, dft_ref[...]).astype(BF)


def _ev_in(tl, xc, xs, mods, ngm, ev_w_in_t, w2f, w2b, b2f, b2b, dftc):
    c0 = H * DK
    assert tl.lead == D // LANE
    ntok = (tl.n_ctx + tl.n_smp) * TM
    widths = [(c0, BF), (c0, BF), (2 * c0, F32), (H * DVP, BF), (H * DVP, BF), (2 * FW, BF)]
    return pl.pallas_call(
        functools.partial(_ev_in_kernel, tl),
        grid=(tl.steps,),
        in_specs=tl.specs(D, True) + [
            tl.mod_spec(), tl.whole(ngm.shape, 0),
            pl.BlockSpec((1, EV_IN, LANE), lambda s: (0, 0, jnp.minimum(s, tl.lead - 1))),
            tl.whole(w2f.shape), tl.whole(w2b.shape), tl.whole(b2f.shape), tl.whole(b2b.shape),
            tl.whole(dftc.shape)],
        out_specs=[tl.specs(w, False)[0] for w, _ in widths],
        out_shape=[jax.ShapeDtypeStruct((ntok, w), dt) for w, dt in widths],
        scratch_shapes=[pltpu.VMEM((D, 2 * c0), BF), pltpu.VMEM((D, 2 * H * DVP), BF),
                        pltpu.VMEM((D, LRP), BF), pltpu.VMEM((D, FW), BF),
                        pltpu.VMEM((LRP, 2 * c0), BF)],
        compiler_params=pltpu.CompilerParams(
            dimension_semantics=("arbitrary",), vmem_limit_bytes=VMEM_LIMIT),
        name="ev_in",
    )(xc, xs, mods, ngm, ev_w_in_t, w2f, w2b, b2f, b2b, dftc)


def _chunk_scan(x, pos, reverse):
    s = 1
    while s < CH:
        if reverse:
            y = pltpu.roll(x, CH - s, axis=0)
            x = x + jnp.where(pos < CH - s, y, 0.0)
        else:
            y = pltpu.roll(x, s, axis=0)
            x = x + jnp.where(pos >= s, y, 0.0)
        s *= 2
    return x


def _gla_kernel(l, has_state, n_cast, pad_wo, with_mod, *refs):
    q_ref, k_ref, la_ref, v_ref, g_ref, xcs_ref, gn_ref, cl_ref, sl_ref = refs[:9]
    refs = refs[9:]
    if has_state:
        s0f_ref, s0b_ref = refs[:2]
        refs = refs[2:]
    cast_src, refs = refs[:n_cast], refs[n_cast:]
    if with_mod:
        mod_in, refs = refs[:4], refs[4:]
    mix_ref, refs = refs[0], refs[1:]
    if not has_state:
        sf_ref, sb_ref = refs[:2]
        refs = refs[2:]
    cast_dst, refs = refs[:n_cast], refs[n_cast:]
    if with_mod:
        _mod_kernel(*mod_in, refs[0])
        refs = refs[1:]
    bc_ref, ke_ref, dec_ref, u_ref, sp_ref, st_ref = refs

    step = pl.program_id(0)
    for j, (src, dst) in enumerate(zip(cast_src, cast_dst)):
        if pad_wo and j == 0:
            rows = src.shape[1]
            per_head = DV // rows

            @pl.when(step == 0)
            def _zero_pad_rows(dst=dst):
                dst[...] = jnp.zeros_like(dst)

            hd = step // per_head
            r0 = jnp.where(step < H * per_head, hd * DVP + (step - hd * per_head) * rows,
                           H * DVP + (step - H * per_head) * rows)
            dst[pl.ds(pl.multiple_of(r0, rows), rows), :] = src[0].astype(BF)
        else:
            dst[...] = src[0].astype(BF)

    nchunks = l // CH
    unroll = min(GLA_UNROLL, nchunks)
    pos = lax.broadcasted_iota(jnp.int32, (CH, DK), 0)
    row = lax.broadcasted_iota(jnp.int32, (CH, CH), 0)
    col = lax.broadcasted_iota(jnp.int32, (CH, CH), 1)
    gn = jnp.concatenate([gn_ref[...], jnp.zeros((1, DVP - DV), F32)], axis=1)

    for hh in range(H):
        ks = slice(hh * DK, (hh + 1) * DK)
        vs = slice(hh * DVP, (hh + 1) * DVP)

        def inc_body(n, carry):
            r0 = pl.multiple_of(n * CH, CH)
            la = la_ref[pl.ds(r0, CH), vs]
            bf = _chunk_scan(la[:, 0:DK], pos, False)
            bb = _chunk_scan(la[:, DK:2 * DK], pos, True)
            bc_ref[pl.ds(r0, CH), 0:DK] = bf
            bc_ref[pl.ds(r0, CH), DK:2 * DK] = bb
            dec_ref[n, :, 0:DK] = jnp.exp(bf[CH - 1:CH, :])
            dec_ref[n, :, DK:2 * DK] = jnp.exp(bb[0:1, :])
            kk = k_ref[pl.ds(r0, CH), ks].astype(F32)
            ke = jnp.concatenate([kk * jnp.exp(-bf), kk * jnp.exp(-bb)], axis=1).astype(BF)
            ke_ref[pl.ds(r0, CH), :] = ke
            u_ref[n] = _dot_tn(v_ref[pl.ds(r0, CH), vs], ke)
            return carry

        lax.fori_loop(0, nchunks, inc_body, 0, unroll=unroll)

        if has_state:
            st_ref[0:DV, 0:DK] = s0f_ref[0, 0, hh]
            st_ref[0:DV, DK:2 * DK] = s0b_ref[0, 0, hh]
            st_ref[DV:DVP, :] = jnp.zeros((DVP - DV, 2 * DK), F32)
        else:
            st_ref[...] = jnp.zeros_like(st_ref)

        def scan_body(i, carry):
            nf = i
            nb = nchunks - 1 - i
            sf = st_ref[:, 0:DK]
            sb = st_ref[:, DK:2 * DK]
            sp_ref[nf, :, 0:DK] = sf.astype(BF)
            sp_ref[nb, :, DK:2 * DK] = sb.astype(BF)
            st_ref[:, 0:DK] = dec_ref[nf, :, 0:DK] * (sf + u_ref[nf, :, 0:DK])
            st_ref[:, DK:2 * DK] = dec_ref[nb, :, DK:2 * DK] * (sb + u_ref[nb, :, DK:2 * DK])
            return carry

        lax.fori_loop(0, nchunks, scan_body, 0)
        if not has_state:
            sf_ref[0, 0, hh] = st_ref[0:DV, 0:DK]
            sb_ref[0, 0, hh] = st_ref[0:DV, DK:2 * DK]

        def out_body(n, carry):
            r0 = pl.multiple_of(n * CH, CH)
            bc = bc_ref[pl.ds(r0, CH), :]
            qq = q_ref[pl.ds(r0, CH), ks].astype(F32)
            qe = jnp.concatenate([qq * jnp.exp(bc[:, 0:DK]), qq * jnp.exp(bc[:, DK:2 * DK])],
                                 axis=1).astype(BF)
            ke = ke_ref[pl.ds(r0, CH), :]
            att = (jnp.where(row >= col, _dot_nt(qe[:, 0:DK], ke[:, 0:DK]), 0.0)
                   + jnp.where(row <= col, _dot_nt(qe[:, DK:], ke[:, DK:]), 0.0)).astype(BF)
            o = _dot(att, v_ref[pl.ds(r0, CH), vs]) + _dot_nt(qe, sp_ref[n])
            ms = jnp.sum(o * o, axis=-1, keepdims=True) * np.float32(1.0 / DV)
            on = o * lax.rsqrt(ms + EPS) * gn
            gate = _silu(g_ref[pl.ds(r0, CH), vs].astype(F32))
            mix_ref[pl.ds(r0, CH), vs] = (on * gate).astype(BF)
            return carry

        lax.fori_loop(0, nchunks, out_body, 0, unroll=unroll)

    fo = _dot(cl_ref[...], xcs_ref[:, 0:FW]) + _dot(sl_ref[...], xcs_ref[:, FW:2 * FW])
    mix_ref[:, H * DVP:H * DVP + FW] = fo.astype(BF)


def _gla(b, l, tok0, q, k, la, v, g, xcs, gn, cl, sl, states, casts, pad_wo=False, mod=None):
    nchunks = l // CH
    c0 = H * DK
    assert tok0 % l == 0
    seq = lambda w: pl.BlockSpec((l, w), lambda i: (tok0 // l + i, 0))
    const = lambda shape: pl.BlockSpec(shape, lambda i: tuple(0 for _ in shape))
    once = lambda shape: pl.BlockSpec(shape, lambda i: (0, 0), pipeline_mode=pl.Buffered(1))
    st_spec = pl.BlockSpec((1, 1, H, DV, DK), lambda i: (i, 0, 0, 0, 0))
    in_specs = [seq(c0), seq(c0), seq(2 * c0), seq(H * DVP), seq(H * DVP), seq(2 * FW),
                const(gn.shape), once((l, l)), once((l, l))]
    out_specs = [pl.BlockSpec((l, MIX0), lambda i: (i, 0))]
    out_shape = [jax.ShapeDtypeStruct((b * l, MIX0), BF)]
    args = [q, k, la, v, g, xcs, gn, cl, sl]
    if states is not None:
        in_specs += [st_spec, st_spec]
        args += list(states)
    else:
        out_specs += [st_spec, st_spec]
        out_shape += [jax.ShapeDtypeStruct((b, 1, H, DV, DK), F32)] * 2
    for j, (w, layer) in enumerate(casts):
        _, r, c = w.shape
        assert r % b == 0
        in_specs.append(pl.BlockSpec((1, r // b, c), lambda i, layer=layer: (layer, i, 0)))
        args.append(w)
        if pad_wo and j == 0:
            assert r == H * DV + FW and DV % (r // b) == 0
            out_specs.append(const((MIX0, c)))
            out_shape.append(jax.ShapeDtypeStruct((MIX0, c), BF))
        else:
            out_specs.append(pl.BlockSpec((r // b, c), lambda i: (i, 0)))
            out_shape.append(jax.ShapeDtypeStruct((r, c), BF))
    if mod is not None:
        mod_in, mod_out, mod_shape = _mod_specs(mod[1], mod[4], b)
        in_specs += mod_in
        args += list(mod[:4])
        out_specs.append(mod_out)
        out_shape.append(mod_shape)
    return pl.pallas_call(
        functools.partial(_gla_kernel, l, states is not None, len(casts), pad_wo, mod is not None),
        grid=(b,),
        in_specs=in_specs,
        out_specs=out_specs,
        out_shape=out_shape,
        scratch_shapes=[pltpu.VMEM((l, 2 * DK), F32),
                        pltpu.VMEM((l, 2 * DK), BF),
                        pltpu.VMEM((nchunks, 1, 2 * DK), F32),
                        pltpu.VMEM((nchunks, DVP, 2 * DK), F32),
                        pltpu.VMEM((nchunks, DVP, 2 * DK), BF),
                        pltpu.VMEM((DVP, 2 * DK), F32)],
        compiler_params=pltpu.CompilerParams(
            dimension_semantics=("arbitrary",), vmem_limit_bytes=VMEM_LIMIT),
        name="gla_fnet",
    )(*args)


def _ffn(x1, mod, ng, w1_ref, w2_ref):
    h = _rms_mod(x1, ng, mod[4], mod[3]).astype(BF)
    wd = D_FF // FF_SPLIT
    out = None
    for c in range(FF_SPLIT):
        hid = jnp.maximum(_dot(h, w1_ref[:, c * wd:(c + 1) * wd]), 0.0)
        part = _dot((hid * hid).astype(BF), w2_ref[c * wd:(c + 1) * wd, :])
        out = part if out is None else out + part
    return x1 + mod[5] * out


def _ev_out_kernel(tl, xc_ref, xs_ref, mc_ref, ms_ref, mod_ref, ng_ref, wo_ref, w1_ref, w2_ref,
                   o_ref):
    s = pl.program_id(0)
    is_ctx = s < tl.n_ctx
    mod = tl.mod_vecs(s, mod_ref)
    x = jnp.where(is_ctx, xc_ref[...], xs_ref[...])
    mix = jnp.where(is_ctx, mc_ref[...], ms_ref[...])
    x1 = x + mod[2] * _dot(mix, wo_ref[...])
    o_ref[...] = _ffn(x1, mod, ng_ref[0], w1_ref, w2_ref)


def _ev_out(tl, xc, xs, mix_c, mix_s, mods, ngf, wo, w1, w2, layer):
    ntok = (tl.n_ctx + tl.n_smp) * TM
    return pl.pallas_call(
        functools.partial(_ev_out_kernel, tl),
        grid=(tl.steps,),
        in_specs=tl.specs(D, True) + tl.specs(MIX0, True) + [
            tl.mod_spec(), tl.whole(ngf.shape, layer),
            tl.weight(wo.shape), tl.weight(w1.shape), tl.weight(w2.shape)],
        out_specs=tl.specs(D, False)[0],
        out_shape=jax.ShapeDtypeStruct((ntok, D), F32),
        compiler_params=pltpu.CompilerParams(
            dimension_semantics=("arbitrary",), vmem_limit_bytes=VMEM_LIMIT),
        name="ev_out_ffn",
    )(xc, xs, mix_c, mix_s, mods, ngf, wo, w1, w2)


def _odd_kernel(tl, ctx_rows, smp_rows, x_ref, mod_ref, ngm_ref, ngf_ref, gfin_ref, ws_ref, gb_ref,
                cw_ref, wi_ref, wo_ref, w1_ref, w2_ref, yc_ref, ys_ref):
    s = pl.program_id(0)
    is_ctx = s < tl.n_ctx
    x = x_ref[...]
    mod = tl.mod_vecs(s, mod_ref)
    h = _rms_mod(x, ngm_ref[0], mod[1], mod[0]).astype(BF)
    v = _gelu_tanh(_dot(h, wi_ref[:, GW:2 * GW])).astype(BF)
    u = _gelu_tanh(_dot(h, wi_ref[:, 0:GW]))
    z = (_dot(h, wi_ref[:, 2 * GW + CW:2 * GW + 2 * CW])
         * _dot(h, wi_ref[:, 2 * GW + 2 * CW:2 * GW + 3 * CW]))
    gate_b = _dot(h, wi_ref[:, 2 * GW:2 * GW + CW])

    sp_rows = []
    for c in range(TM // GCH):
        cols = []
        for gi in range(GW // GCH):
            vg = v[c * GCH:(c + 1) * GCH, gi * GCH:(gi + 1) * GCH]
            cols.append(_dot(ws_ref[0, gi].astype(BF), vg))
        sp_rows.append(jnp.concatenate(cols, axis=1) + gb_ref[...])
    out_c = u * jnp.concatenate(sp_rows, axis=0)

    last = jnp.where(is_ctx, ctx_rows - 1, smp_rows - 1)
    pos = lax.broadcasted_iota(jnp.int32, (TM, CW), 0) & last
    zl = jnp.where(pos >= 1, pltpu.roll(z, 1, axis=0), 0.0)
    zr = jnp.where(pos < last, pltpu.roll(z, TM - 1, axis=0), 0.0)
    cw = cw_ref[0]
    out_d = gate_b * (zl * cw[0:1] + z * cw[1:2] + zr * cw[2:3])

    mix = jnp.concatenate([out_c, out_d], axis=1).astype(BF)
    x1 = x + mod[2] * _dot(mix, wo_ref[...])
    x2 = _ffn(x1, mod, ngf_ref[0], w1_ref, w2_ref)
    ms = jnp.mean(x2 * x2, axis=-1, keepdims=True)
    y = x2 * lax.rsqrt(ms + EPS) * gfin_ref[...]

    @pl.when(is_ctx)
    def _store_ctx():
        yc_ref[...] = y

    @pl.when(jnp.logical_not(is_ctx))
    def _store_smp():
        ys_ref[...] = y


def _odd(tl, ctx_rows, smp_rows, x, mods, ngm, ngf, gfin, gmlp_ws, gbias, conv_w, wi, wo, w1, w2,
         layer):
    for r in (ctx_rows, smp_rows):
        assert TM % r == 0 and r & (r - 1) == 0
    assert TM % GCH == 0
    return pl.pallas_call(
        functools.partial(_odd_kernel, tl, ctx_rows, smp_rows),
        grid=(tl.steps,),
        in_specs=tl.specs(D, False) + [
            tl.mod_spec(), tl.whole(ngm.shape, layer), tl.whole(ngf.shape, layer),
            tl.whole(gfin.shape), tl.whole(gmlp_ws.shape), tl.whole(gbias.shape),
            tl.whole(conv_w.shape),
            tl.weight(wi.shape), tl.weight(wo.shape), tl.weight(w1.shape), tl.weight(w2.shape)],
        out_specs=tl.specs(D, True),
        out_shape=[jax.ShapeDtypeStruct((tl.n_ctx * TM, D), F32),
                   jax.ShapeDtypeStruct((tl.n_smp * TM, D), F32)],
        compiler_params=pltpu.CompilerParams(
            dimension_semantics=("arbitrary",), vmem_limit_bytes=VMEM_LIMIT),
        name="odd_ffn_final",
    )(x, mods, ngm, ngf, gfin, gmlp_ws, gbias, conv_w, wi, wo, w1, w2)


def _dft_pos(l):
    n = np.arange(l)
    ang = 2.0 * np.pi * ((n[:, None] * n[None, :]) % l) / l
    to_bf = lambda a: jnp.asarray(a, F32).astype(BF)
    return to_bf(np.cos(ang) / np.sqrt(l)), to_bf(-np.sin(ang) / np.sqrt(l))


def _dft_chan():
    m = np.arange(FG)
    angc = 2.0 * np.pi * ((m[:, None] * m[None, :]) % FG) / FG
    eye = np.eye(FW // FG)
    cc = np.kron(eye, np.cos(angc)) / np.sqrt(FG)
    sc = np.kron(eye, np.sin(angc)) / np.sqrt(FG)
    return jnp.asarray(np.concatenate([cc, sc], axis=1), F32).astype(BF)


def kernel(x_prompt, x_sample, state_gla_fwd, state_gla_bwd, c, c_ctx, ada_w, ada_b, norm_mix_g,
           norm_ffn_g, ffn_w1, ffn_w2, ev_w_in, ev_w_out, gla_w2_f, gla_b2_f, gla_w2_b, gla_b2_b,
           gla_norm_g, od_w_in, od_w_out, gmlp_ws, gmlp_b, conv_w, final_norm_g):
    b_ctx, l_ctx, _ = x_prompt.shape
    b_smp, l_smp, _ = x_sample.shape
    depth = ada_w.shape[0]
    assert TM % l_ctx == 0 and l_smp % TM == 0 and b_smp % SUB == 0
    n_ctx, n_smp, tps = b_ctx * l_ctx // TM, b_smp * l_smp // TM, l_smp // TM
    tl = _Tiles(n_ctx, n_smp, tps)

    mod_args = (c_ctx.reshape(1, D), c, ada_w, ada_b.reshape(depth, 1, N_MOD * D))
    mods0 = _modulation(*mod_args, 0)
    ngm = norm_mix_g.reshape(depth, 1, D)
    ngf = norm_ffn_g.reshape(depth, 1, D)
    gbias = jnp.repeat(gmlp_b[0].T, GCH, axis=1)
    xc = x_prompt.reshape(b_ctx * l_ctx, D)
    xs = x_sample.reshape(b_smp * l_smp, D)

    q, k, la, v, g, xcs = _ev_in(_Tiles(n_ctx, n_smp, tps, lead=D // LANE), xc, xs, mods0, ngm,
                                 jnp.swapaxes(ev_w_in, 1, 2), gla_w2_f, gla_w2_b, gla_b2_f,
                                 gla_b2_b, _dft_chan())
    mix_c, sf, sb, wo0, w1_0, w2_0, mods1 = _gla(
        b_ctx, l_ctx, 0, q, k, la, v, g, xcs, gla_norm_g, *_dft_pos(l_ctx), None,
        [(ev_w_out, 0), (ffn_w1, 0), (ffn_w2, 0)], pad_wo=True, mod=mod_args + (1,))
    mix_s, wi1, wo1, w1_1, w2_1 = _gla(
        b_smp, l_smp, b_ctx * l_ctx, q, k, la, v, g, xcs, gla_norm_g, *_dft_pos(l_smp),
        (jnp.swapaxes(state_gla_fwd, -1, -2), jnp.swapaxes(state_gla_bwd, -1, -2)),
        [(od_w_in, 0), (od_w_out, 0), (ffn_w1, 1), (ffn_w2, 1)])
    x1 = _ev_out(tl, xc, xs, mix_c, mix_s, mods0, ngf, wo0, w1_0, w2_0, 0)

    yc, ys = _odd(tl, l_ctx, GRID_W, x1, mods1, ngm, ngf, final_norm_g.reshape(1, D), gmlp_ws, gbias,
                  conv_w, wi1, wo1, w1_1, w2_1, 1)
    return (yc.reshape(b_ctx, l_ctx, D), ys.reshape(b_smp, l_smp, D),
            jnp.swapaxes(sf, -1, -2), jnp.swapaxes(sb, -1, -2))
```

```python
import functools

import jax
import jax.numpy as jnp
import numpy as np
from jax import lax
from jax.experimental import pallas as pl
from jax.experimental.pallas import tpu as pltpu

D = 1024
D_FF = 4 * D
EPS = 1e-6
N_MOD = 6
GRID_W = 64

H = 4
DK = 128
DV = 192
GLA_LOWRANK = 16
GLA_TAU = 16.0
CH = 64
FW = 256
FG = 64
GW = 512
GCH = 128
CW = 512

EV_V0 = 2 * H * DK
EV_G0 = EV_V0 + H * DV
EV_LR0 = EV_G0 + H * DV
EV_FIN0 = EV_LR0 + 2 * GLA_LOWRANK
EV_IN = EV_FIN0 + FW

TM = 512
FF_SPLIT = 2
LRP = 128
PG = 256
MIX0 = H * DV + FW
OD_COLS = 2 * GW + 3 * CW
GLA_UNROLL = 16
CTX_SEQS = 2
SUB = 8
LANE = 128

VMEM_LIMIT = 60 * 1024 * 1024

BF = jnp.bfloat16
F32 = jnp.float32


def _dot(a, b):
    return jnp.dot(a, b, preferred_element_type=F32)


def _dot_nt(a, b):
    return lax.dot_general(a, b, (((1,), (1,)), ((), ())), preferred_element_type=F32)


def _dot_tn(a, b):
    return lax.dot_general(a, b, (((0,), (0,)), ((), ())), preferred_element_type=F32)


def _sigmoid(x):
    return 1.0 / (1.0 + jnp.exp(-x))


def _silu(x):
    return x * _sigmoid(x)


def _log_sigmoid(x):
    return jnp.minimum(x, 0.0) - jnp.log(1.0 + jnp.exp(-jnp.abs(x)))


def _gelu_tanh(x):
    c = np.float32(np.sqrt(2.0 / np.pi))
    return 0.5 * x * (1.0 + jnp.tanh(c * (x + 0.044715 * (x * x * x))))


def _rms_mod(x, g, sc, sh):
    ms = jnp.mean(x * x, axis=-1, keepdims=True)
    return (x * lax.rsqrt(ms + EPS)) * (g * (1.0 + sc)) + sh


def _pair_heads(f, b):
    return jnp.concatenate([p for hh in range(H)
                            for p in (f[:, hh * DK:(hh + 1) * DK], b[:, hh * DK:(hh + 1) * DK])], axis=1)


class _Tiles:
    def __init__(self, n_ctx, n_smp, tiles_per_smp_seq, lead=0):
        self.n_ctx, self.n_smp, self.tps, self.lead = n_ctx, n_smp, tiles_per_smp_seq, lead
        self.steps = lead + n_ctx + n_smp

    def tile(self, s):
        return jnp.maximum(s - self.lead, 0)

    def ctx(self, s):
        return jnp.minimum(self.tile(s), self.n_ctx - 1)

    def smp(self, s):
        return jnp.maximum(self.tile(s) - self.n_ctx, 0)

    def cond_row(self, s):
        return jnp.where(self.tile(s) < self.n_ctx, 0, SUB + self.smp(s) // self.tps)

    def specs(self, width, dual):
        if dual:
            return [pl.BlockSpec((TM, width), lambda s: (self.ctx(s), 0)),
                    pl.BlockSpec((TM, width), lambda s: (self.smp(s), 0))]
        return [pl.BlockSpec((TM, width), lambda s: (self.tile(s), 0))]

    def mod_spec(self):
        return pl.BlockSpec((1, SUB, N_MOD * D), lambda s: (0, self.cond_row(s) // SUB, 0))

    def mod_vecs(self, s, mod_ref):
        row = mod_ref[0, pl.ds(self.cond_row(s) % SUB, 1), :]
        return [row[:, i * D:(i + 1) * D] for i in range(N_MOD)]

    def whole(self, shape, layer=None):
        if layer is None:
            return pl.BlockSpec(shape, lambda s: tuple(0 for _ in shape))
        return pl.BlockSpec((1,) + shape[1:], lambda s: (layer,) + tuple(0 for _ in shape[1:]))

    def weight(self, shape):
        return pl.BlockSpec(shape, lambda s: (0, 0), pipeline_mode=pl.Buffered(1))


def _mod_kernel(cx_ref, c_ref, w_ref, b_ref, o_ref):
    w = w_ref[0].astype(BF)
    ax = jnp.broadcast_to(_silu(cx_ref[...]), (SUB, D)).astype(BF)
    o_ref[0, 0:SUB, :] = _dot(ax, w) + b_ref[0]
    o_ref[0, SUB:, :] = _dot(_silu(c_ref[...]).astype(BF), w) + b_ref[0]


def _mod_specs(c, layer, steps):
    tn = N_MOD * D // steps
    assert tn % LANE == 0
    return ([pl.BlockSpec((1, D), lambda j: (0, 0)),
             pl.BlockSpec(c.shape, lambda j: (0, 0)),
             pl.BlockSpec((1, D, tn), lambda j: (layer, 0, j)),
             pl.BlockSpec((1, 1, tn), lambda j: (layer, 0, j))],
            pl.BlockSpec((1, SUB + c.shape[0], tn), lambda j: (0, 0, j)),
            jax.ShapeDtypeStruct((1, SUB + c.shape[0], N_MOD * D), F32))


def _modulation(c_ctx, c, ada_w, ada_b, layer):
    steps = N_MOD
    in_specs, out_spec, out_shape = _mod_specs(c, layer, steps)
    return pl.pallas_call(
        _mod_kernel,
        grid=(steps,),
        in_specs=in_specs,
        out_specs=out_spec,
        out_shape=out_shape,
        compiler_params=pltpu.CompilerParams(
            dimension_semantics=("arbitrary",), vmem_limit_bytes=VMEM_LIMIT),
        name="modulation",
    )(c_ctx, c, ada_w, ada_b)


def _ev_in_kernel(tl, xc_ref, xs_ref, mod_ref, ng_ref, wc_ref, w2f_ref, w2b_ref, b2f_ref, b2b_ref,
                  dft_ref, q_ref, k_ref, la_ref, v_ref, g_ref, xcs_ref,
                  wqk_ref, wvg_ref, wlr_ref, wfin_ref, w2_ref):
    s = pl.program_id(0)
    c0 = H * DK

    @pl.when(s == 0)
    def _gate_weights():
        z = jnp.zeros((GLA_LOWRANK, c0), F32)
        w2_ref[...] = jnp.zeros_like(w2_ref)
        w2_ref[0:GLA_LOWRANK, :] = _pair_heads(w2f_ref[0], z).astype(BF)
        w2_ref[GLA_LOWRANK:2 * GLA_LOWRANK, :] = _pair_heads(z, w2b_ref[0]).astype(BF)

    @pl.when(s < tl.lead)
    def _cast_weights():
        w = wc_ref[0]
        r0 = pl.multiple_of(s * LANE, LANE)
        rows = pl.ds(r0, LANE)

        def tr(lo, hi, pad):
            piece = w[lo:hi, :]
            if pad:
                piece = jnp.concatenate([piece, jnp.zeros((pad, LANE), F32)], axis=0)
            return piece.T.astype(BF)

        wqk_ref[rows, :] = tr(0, EV_V0, 0)
        wvg_ref[rows, :] = tr(EV_V0, EV_LR0, 0)
        wlr_ref[rows, :] = tr(EV_LR0, EV_FIN0, LRP - 2 * GLA_LOWRANK)
        wfin_ref[rows, :] = tr(EV_FIN0, EV_IN, 0)

    @pl.when(s >= tl.lead)
    def _tile():
        x = jnp.where(s - tl.lead < tl.n_ctx, xc_ref[...], xs_ref[...])
        mod = tl.mod_vecs(s, mod_ref)
        h = _rms_mod(x, ng_ref[0], mod[1], mod[0]).astype(BF)

        lr = _dot(h, wlr_ref[...]).astype(BF)
        b2 = _pair_heads(b2f_ref[...], b2b_ref[...])
        nv = H * DV // PG
        for j in range(2 * nv):
            cs = slice(j * PG, (j + 1) * PG)
            if j < 2 * c0 // PG:
                pre = _dot(lr, w2_ref[:, cs]) + b2[:, cs]
                la_ref[:, cs] = _log_sigmoid(pre) * np.float32(1.0 / GLA_TAU)
            out_ref, oc = (v_ref, j) if j < nv else (g_ref, j - nv)
            out_ref[:, oc * PG:(oc + 1) * PG] = _dot(h, wvg_ref[:, cs]).astype(BF)
        q_ref[...] = (_dot(h, wqk_ref[:, 0:c0]) * np.float32(DK ** -0.5)).astype(BF)
        k_ref[...] = _dot(h, wqk_ref[:, c0:2 * c0]).astype(BF)
        fin = _dot(h, wfin_ref[...]).astype(BF)
        xcs_ref[...] = _dot(fin, dft_ref[...]).astype(BF)


def _ev_in(tl, xc, xs, mods, ngm, ev_w_in_t, w2f, w2b, b2f, b2b, dftc):
    c0 = H * DK
    assert tl.lead == D // LANE and (H * DV) % PG == 0 and 2 * DK == PG
    ntok = (tl.n_ctx + tl.n_smp) * TM
    widths = [(c0, BF), (c0, BF), (2 * c0, F32), (H * DV, BF), (H * DV, BF), (2 * FW, BF)]
    return pl.pallas_call(
        functools.partial(_ev_in_kernel, tl),
        grid=(tl.steps,),
        in_specs=tl.specs(D, True) + [
            tl.mod_spec(), tl.whole(ngm.shape, 0),
            pl.BlockSpec((1, EV_IN, LANE), lambda s: (0, 0, jnp.minimum(s, tl.lead - 1))),
            tl.whole(w2f.shape), tl.whole(w2b.shape), tl.whole(b2f.shape), tl.whole(b2b.shape),
            tl.whole(dftc.shape)],
        out_specs=[tl.specs(w, False)[0] for w, _ in widths],
        out_shape=[jax.ShapeDtypeStruct((ntok, w), dt) for w, dt in widths],
        scratch_shapes=[pltpu.VMEM((D, 2 * c0), BF), pltpu.VMEM((D, 2 * H * DV), BF),
                        pltpu.VMEM((D, LRP), BF), pltpu.VMEM((D, FW), BF),
                        pltpu.VMEM((LRP, 2 * c0), BF)],
        compiler_params=pltpu.CompilerParams(
            dimension_semantics=("arbitrary",), vmem_limit_bytes=VMEM_LIMIT),
        name="ev_in",
    )(xc, xs, mods, ngm, ev_w_in_t, w2f, w2b, b2f, b2b, dftc)


def _chunk_scan(x, pos, reverse):
    s = 1
    while s < CH:
        if reverse:
            y = pltpu.roll(x, CH - s, axis=0)
            x = x + jnp.where(pos < CH - s, y, 0.0)
        else:
            y = pltpu.roll(x, s, axis=0)
            x = x + jnp.where(pos >= s, y, 0.0)
        s *= 2
    return x


def _gla_kernel(l, nseq, has_state, n_cast, with_mod, *refs):
    q_ref, k_ref, la_ref, v_ref, g_ref, xcs_ref, gn_ref, cl_ref, sl_ref = refs[:9]
    refs = refs[9:]
    if has_state:
        s0f_ref, s0b_ref = refs[:2]
        refs = refs[2:]
    cast_src, refs = refs[:n_cast], refs[n_cast:]
    if with_mod:
        mod_in, refs = refs[:4], refs[4:]
    mix_ref, refs = refs[0], refs[1:]
    if not has_state:
        sf_ref, sb_ref = refs[:2]
        refs = refs[2:]
    cast_dst, refs = refs[:n_cast], refs[n_cast:]
    if with_mod:
        _mod_kernel(*mod_in, refs[0])
        refs = refs[1:]
    bc_ref, ke_ref, dec_ref, u_ref, sp_ref, st_ref = refs

    for src, dst in zip(cast_src, cast_dst):
        dst[...] = src[0].astype(BF)

    nchunks = l // CH
    nall = nseq * nchunks
    unroll = min(GLA_UNROLL, nall)
    pos = lax.broadcasted_iota(jnp.int32, (CH, DK), 0)
    row = lax.broadcasted_iota(jnp.int32, (CH, CH), 0)
    col = lax.broadcasted_iota(jnp.int32, (CH, CH), 1)
    gn = gn_ref[...]

    for hh in range(H):
        ks = slice(hh * DK, (hh + 1) * DK)
        vs = slice(hh * DV, (hh + 1) * DV)

        def inc_body(n, carry):
            r0 = pl.multiple_of(n * CH, CH)
            la = la_ref[pl.ds(r0, CH), hh * 2 * DK:(hh + 1) * 2 * DK]
            bf = _chunk_scan(la[:, 0:DK], pos, False)
            bb = _chunk_scan(la[:, DK:2 * DK], pos, True)
            bc_ref[pl.ds(r0, CH), 0:DK] = bf
            bc_ref[pl.ds(r0, CH), DK:2 * DK] = bb
            dec_ref[n, :, 0:DK] = jnp.exp(bf[CH - 1:CH, :])
            dec_ref[n, :, DK:2 * DK] = jnp.exp(bb[0:1, :])
            kk = k_ref[pl.ds(r0, CH), ks].astype(F32)
            ke = jnp.concatenate([kk * jnp.exp(-bf), kk * jnp.exp(-bb)], axis=1).astype(BF)
            ke_ref[pl.ds(r0, CH), :] = ke
            u_ref[n] = _dot_tn(v_ref[pl.ds(r0, CH), vs], ke)
            return carry

        lax.fori_loop(0, nall, inc_body, 0, unroll=unroll)

        for sq in range(nseq):
            if has_state:
                st_ref[:, 0:DK] = s0f_ref[sq, 0, hh]
                st_ref[:, DK:2 * DK] = s0b_ref[sq, 0, hh]
            else:
                st_ref[...] = jnp.zeros_like(st_ref)

            def scan_body(i, carry, sq=sq):
                nf = sq * nchunks + i
                nb = sq * nchunks + nchunks - 1 - i
                sf = st_ref[:, 0:DK]
                sb = st_ref[:, DK:2 * DK]
                sp_ref[nf, :, 0:DK] = sf.astype(BF)
                sp_ref[nb, :, DK:2 * DK] = sb.astype(BF)
                st_ref[:, 0:DK] = dec_ref[nf, :, 0:DK] * (sf + u_ref[nf, :, 0:DK])
                st_ref[:, DK:2 * DK] = dec_ref[nb, :, DK:2 * DK] * (sb + u_ref[nb, :, DK:2 * DK])
                return carry

            lax.fori_loop(0, nchunks, scan_body, 0)
            if not has_state:
                sf_ref[sq, 0, hh] = st_ref[:, 0:DK]
                sb_ref[sq, 0, hh] = st_ref[:, DK:2 * DK]

        def out_body(n, carry):
            r0 = pl.multiple_of(n * CH, CH)
            bc = bc_ref[pl.ds(r0, CH), :]
            qq = q_ref[pl.ds(r0, CH), ks].astype(F32)
            qe = jnp.concatenate([qq * jnp.exp(bc[:, 0:DK]), qq * jnp.exp(bc[:, DK:2 * DK])],
                                 axis=1).astype(BF)
            ke = ke_ref[pl.ds(r0, CH), :]
            att = (jnp.where(row >= col, _dot_nt(qe[:, 0:DK], ke[:, 0:DK]), 0.0)
                   + jnp.where(row <= col, _dot_nt(qe[:, DK:], ke[:, DK:]), 0.0)).astype(BF)
            o = _dot(att, v_ref[pl.ds(r0, CH), vs]) + _dot_nt(qe, sp_ref[n])
            ms = jnp.mean(o * o, axis=-1, keepdims=True)
            on = o * lax.rsqrt(ms + EPS) * gn
            gate = _silu(g_ref[pl.ds(r0, CH), vs].astype(F32))
            mix_ref[pl.ds(r0, CH), vs] = (on * gate).astype(BF)
            return carry

        lax.fori_loop(0, nall, out_body, 0, unroll=unroll)

    for sq in range(nseq):
        rs = slice(sq * l, (sq + 1) * l)
        fo = _dot(cl_ref[...], xcs_ref[rs, 0:FW]) + _dot(sl_ref[...], xcs_ref[rs, FW:2 * FW])
        mix_ref[rs, H * DV:H * DV + FW] = fo.astype(BF)


def _gla(b, l, nseq, tok0, q, k, la, v, g, xcs, gn, cl, sl, states, casts, mod=None):
    nchunks = l // CH
    c0 = H * DK
    rows = nseq * l
    steps = b // nseq
    assert tok0 % rows == 0 and b % nseq == 0
    seq = lambda w: pl.BlockSpec((rows, w), lambda i: (tok0 // rows + i, 0))
    const = lambda shape: pl.BlockSpec(shape, lambda i: tuple(0 for _ in shape))
    once = lambda shape: pl.BlockSpec(shape, lambda i: (0, 0), pipeline_mode=pl.Buffered(1))
    st_spec = pl.BlockSpec((nseq, 1, H, DV, DK), lambda i: (i, 0, 0, 0, 0))
    in_specs = [seq(c0), seq(c0), seq(2 * c0), seq(H * DV), seq(H * DV), seq(2 * FW),
                const(gn.shape), once((l, l)), once((l, l))]
    out_specs = [pl.BlockSpec((rows, MIX0), lambda i: (i, 0))]
    out_shape = [jax.ShapeDtypeStruct((b * l, MIX0), BF)]
    args = [q, k, la, v, g, xcs, gn, cl, sl]
    if states is not None:
        in_specs += [st_spec, st_spec]
        args += list(states)
    else:
        out_specs += [st_spec, st_spec]
        out_shape += [jax.ShapeDtypeStruct((b, 1, H, DV, DK), F32)] * 2
    for w, layer in casts:
        _, r, c = w.shape
        assert r % steps == 0
        in_specs.append(pl.BlockSpec((1, r // steps, c), lambda i, layer=layer: (layer, i, 0)))
        args.append(w)
        out_specs.append(pl.BlockSpec((r // steps, c), lambda i: (i, 0)))
        out_shape.append(jax.ShapeDtypeStruct((r, c), BF))
    if mod is not None:
        mod_in, mod_out, mod_shape = _mod_specs(mod[1], mod[4], steps)
        in_specs += mod_in
        args += list(mod[:4])
        out_specs.append(mod_out)
        out_shape.append(mod_shape)
    return pl.pallas_call(
        functools.partial(_gla_kernel, l, nseq, states is not None, len(casts), mod is not None),
        grid=(steps,),
        in_specs=in_specs,
        out_specs=out_specs,
        out_shape=out_shape,
        scratch_shapes=[pltpu.VMEM((rows, 2 * DK), F32),
                        pltpu.VMEM((rows, 2 * DK), BF),
                        pltpu.VMEM((nseq * nchunks, 1, 2 * DK), F32),
                        pltpu.VMEM((nseq * nchunks, DV, 2 * DK), F32),
                        pltpu.VMEM((nseq * nchunks, DV, 2 * DK), BF),
                        pltpu.VMEM((DV, 2 * DK), F32)],
        compiler_params=pltpu.CompilerParams(
            dimension_semantics=("arbitrary",), vmem_limit_bytes=VMEM_LIMIT),
        name="gla_fnet",
    )(*args)


def _ffn(x1, mod, ng, w1_ref, w2_ref):
    h = _rms_mod(x1, ng, mod[4], mod[3]).astype(BF)
    wd = D_FF // FF_SPLIT
    out = None
    for c in range(FF_SPLIT):
        hid = jnp.maximum(_dot(h, w1_ref[:, c * wd:(c + 1) * wd]), 0.0)
        part = _dot((hid * hid).astype(BF), w2_ref[c * wd:(c + 1) * wd, :])
        out = part if out is None else out + part
    return x1 + mod[5] * out


def _ev_out_kernel(tl, xc_ref, xs_ref, mc_ref, ms_ref, mod_ref, ng_ref, wo_ref, w1_ref, w2_ref,
                   o_ref):
    s = pl.program_id(0)
    is_ctx = s < tl.n_ctx
    mod = tl.mod_vecs(s, mod_ref)
    x = jnp.where(is_ctx, xc_ref[...], xs_ref[...])
    mix = jnp.where(is_ctx, mc_ref[...], ms_ref[...])
    x1 = x + mod[2] * _dot(mix, wo_ref[...])
    o_ref[...] = _ffn(x1, mod, ng_ref[0], w1_ref, w2_ref)


def _ev_out(tl, xc, xs, mix_c, mix_s, mods, ngf, wo, w1, w2, layer):
    ntok = (tl.n_ctx + tl.n_smp) * TM
    return pl.pallas_call(
        functools.partial(_ev_out_kernel, tl),
        grid=(tl.steps,),
        in_specs=tl.specs(D, True) + tl.specs(MIX0, True) + [
            tl.mod_spec(), tl.whole(ngf.shape, layer),
            tl.weight(wo.shape), tl.weight(w1.shape), tl.weight(w2.shape)],
        out_specs=tl.specs(D, False)[0],
        out_shape=jax.ShapeDtypeStruct((ntok, D), F32),
        compiler_params=pltpu.CompilerParams(
            dimension_semantics=("arbitrary",), vmem_limit_bytes=VMEM_LIMIT),
        name="ev_out_ffn",
    )(xc, xs, mix_c, mix_s, mods, ngf, wo, w1, w2)


def _odd_kernel(tl, ctx_rows, smp_rows, x_ref, mod_ref, ngm_ref, ngf_ref, gfin_ref, ws_ref, gb_ref,
                cw_ref, wi_ref, wo_ref, w1_ref, w2_ref, yc_ref, ys_ref):
    s = pl.program_id(0)
    is_ctx = s < tl.n_ctx
    x = x_ref[...]
    mod = tl.mod_vecs(s, mod_ref)
    h = _rms_mod(x, ngm_ref[0], mod[1], mod[0]).astype(BF)
    v = _gelu_tanh(_dot(h, wi_ref[:, GW:2 * GW])).astype(BF)
    u = _gelu_tanh(_dot(h, wi_ref[:, 0:GW]))
    z = (_dot(h, wi_ref[:, 2 * GW + CW:2 * GW + 2 * CW])
         * _dot(h, wi_ref[:, 2 * GW + 2 * CW:2 * GW + 3 * CW]))
    gate_b = _dot(h, wi_ref[:, 2 * GW:2 * GW + CW])

    sp_rows = []
    for c in range(TM // GCH):
        cols = []
        for gi in range(GW // GCH):
            vg = v[c * GCH:(c + 1) * GCH, gi * GCH:(gi + 1) * GCH]
            cols.append(_dot(ws_ref[0, gi].astype(BF), vg))
        sp_rows.append(jnp.concatenate(cols, axis=1) + gb_ref[...])
    out_c = u * jnp.concatenate(sp_rows, axis=0)

    last = jnp.where(is_ctx, ctx_rows - 1, smp_rows - 1)
    pos = lax.broadcasted_iota(jnp.int32, (TM, CW), 0) & last
    zl = jnp.where(pos >= 1, pltpu.roll(z, 1, axis=0), 0.0)
    zr = jnp.where(pos < last, pltpu.roll(z, TM - 1, axis=0), 0.0)
    cw = cw_ref[0]
    out_d = gate_b * (zl * cw[0:1] + z * cw[1:2] + zr * cw[2:3])

    mix = jnp.concatenate([out_c, out_d], axis=1).astype(BF)
    x1 = x + mod[2] * _dot(mix, wo_ref[...])
    x2 = _ffn(x1, mod, ngf_ref[0], w1_ref, w2_ref)
    ms = jnp.mean(x2 * x2, axis=-1, keepdims=True)
    y = x2 * lax.rsqrt(ms + EPS) * gfin_ref[...]

    @pl.when(is_ctx)
    def _store_ctx():
        yc_ref[...] = y

    @pl.when(jnp.logical_not(is_ctx))
    def _store_smp():
        ys_ref[...] = y


def _odd(tl, ctx_rows, smp_rows, x, mods, ngm, ngf, gfin, gmlp_ws, gbias, conv_w, wi, wo, w1, w2,
         layer):
    for r in (ctx_rows, smp_rows):
        assert TM % r == 0 and r & (r - 1) == 0
    assert TM % GCH == 0
    return pl.pallas_call(
        functools.partial(_odd_kernel, tl, ctx_rows, smp_rows),
        grid=(tl.steps,),
        in_specs=tl.specs(D, False) + [
            tl.mod_spec(), tl.whole(ngm.shape, layer), tl.whole(ngf.shape, layer),
            tl.whole(gfin.shape), tl.whole(gmlp_ws.shape), tl.whole(gbias.shape),
            tl.whole(conv_w.shape),
            tl.weight(wi.shape), tl.weight(wo.shape), tl.weight(w1.shape), tl.weight(w2.shape)],
        out_specs=tl.specs(D, True),
        out_shape=[jax.ShapeDtypeStruct((tl.n_ctx * TM, D), F32),
                   jax.ShapeDtypeStruct((tl.n_smp * TM, D), F32)],
        compiler_params=pltpu.CompilerParams(
            dimension_semantics=("arbitrary",), vmem_limit_bytes=VMEM_LIMIT),
        name="odd_ffn_final",
    )(x, mods, ngm, ngf, gfin, gmlp_ws, gbias, conv_w, wi, wo, w1, w2)


def _dft_pos(l):
    n = np.arange(l)
    ang = 2.0 * np.pi * ((n[:, None] * n[None, :]) % l) / l
    to_bf = lambda a: jnp.asarray(a, F32).astype(BF)
    return to_bf(np.cos(ang) / np.sqrt(l)), to_bf(-np.sin(ang) / np.sqrt(l))


def _dft_chan():
    m = np.arange(FG)
    angc = 2.0 * np.pi * ((m[:, None] * m[None, :]) % FG) / FG
    eye = np.eye(FW // FG)
    cc = np.kron(eye, np.cos(angc)) / np.sqrt(FG)
    sc = np.kron(eye, np.sin(angc)) / np.sqrt(FG)
    return jnp.asarray(np.concatenate([cc, sc], axis=1), F32).astype(BF)


def kernel(x_prompt, x_sample, state_gla_fwd, state_gla_bwd, c, c_ctx, ada_w, ada_b, norm_mix_g,
           norm_ffn_g, ffn_w1, ffn_w2, ev_w_in, ev_w_out, gla_w2_f, gla_b2_f, gla_w2_b, gla_b2_b,
           gla_norm_g, od_w_in, od_w_out, gmlp_ws, gmlp_b, conv_w, final_norm_g):
    b_ctx, l_ctx, _ = x_prompt.shape
    b_smp, l_smp, _ = x_sample.shape
    depth = ada_w.shape[0]
    assert TM % l_ctx == 0 and l_smp % TM == 0 and b_smp % SUB == 0
    n_ctx, n_smp, tps = b_ctx * l_ctx // TM, b_smp * l_smp // TM, l_smp // TM
    tl = _Tiles(n_ctx, n_smp, tps)

    mod_args = (c_ctx.reshape(1, D), c, ada_w, ada_b.reshape(depth, 1, N_MOD * D))
    mods0 = _modulation(*mod_args, 0)
    ngm = norm_mix_g.reshape(depth, 1, D)
    ngf = norm_ffn_g.reshape(depth, 1, D)
    gbias = jnp.repeat(gmlp_b[0].T, GCH, axis=1)
    xc = x_prompt.reshape(b_ctx * l_ctx, D)
    xs = x_sample.reshape(b_smp * l_smp, D)

    q, k, la, v, g, xcs = _ev_in(_Tiles(n_ctx, n_smp, tps, lead=D // LANE), xc, xs, mods0, ngm,
                                 jnp.swapaxes(ev_w_in, 1, 2), gla_w2_f, gla_w2_b, gla_b2_f,
                                 gla_b2_b, _dft_chan())
    mix_c, sf, sb, wo0, w1_0, w2_0, mods1 = _gla(
        b_ctx, l_ctx, CTX_SEQS, 0, q, k, la, v, g, xcs, gla_norm_g, *_dft_pos(l_ctx), None,
        [(ev_w_out, 0), (ffn_w1, 0), (ffn_w2, 0)], mod=mod_args + (1,))
    mix_s, wi1, wo1, w1_1, w2_1 = _gla(
        b_smp, l_smp, 1, b_ctx * l_ctx, q, k, la, v, g, xcs, gla_norm_g, *_dft_pos(l_smp),
        (jnp.swapaxes(state_gla_fwd, -1, -2), jnp.swapaxes(state_gla_bwd, -1, -2)),
        [(od_w_in, 0), (od_w_out, 0), (ffn_w1, 1), (ffn_w2, 1)])
    x1 = _ev_out(tl, xc, xs, mix_c, mix_s, mods0, ngf, wo0, w1_0, w2_0, 0)

    yc, ys = _odd(tl, l_ctx, GRID_W, x1, mods1, ngm, ngf, final_norm_g.reshape(1, D), gmlp_ws, gbias,
                  conv_w, wi1, wo1, w1_1, w2_1, 1)
    return (yc.reshape(b_ctx, l_ctx, D), ys.reshape(b_smp, l_smp, D),
            jnp.swapaxes(sf, -1, -2), jnp.swapaxes(sb, -1, -2))
```

```python
import functools

import jax
import jax.numpy as jnp
import numpy as np
from jax import lax
from jax.experimental import pallas as pl
from jax.experimental.pallas import tpu as pltpu

D = 1024
D_FF = 4 * D
EPS = 1e-6
N_MOD = 6
GRID_W = 64

H = 4
DK = 128
DV = 192
GLA_LOWRANK = 16
GLA_TAU = 16.0
CH = 64
FW = 256
FG = 64
GW = 512
GCH = 128
CW = 512

EV_V0 = 2 * H * DK
EV_G0 = EV_V0 + H * DV
EV_LR0 = EV_G0 + H * DV
EV_FIN0 = EV_LR0 + 2 * GLA_LOWRANK
EV_IN = EV_FIN0 + FW

TM = 512
FF_SPLIT = 2
LRP = 128
PG = 256
MIX0 = H * DV + FW
OD_COLS = 2 * GW + 3 * CW
GLA_UNROLL = 16
CTX_SEQS = 2
CAST_STEPS = 16
SUB = 8
LANE = 128

VMEM_LIMIT = 60 * 1024 * 1024

BF = jnp.bfloat16
F32 = jnp.float32


def _dot(a, b):
    return jnp.dot(a, b, preferred_element_type=F32)


def _dot_nt(a, b):
    return lax.dot_general(a, b, (((1,), (1,)), ((), ())), preferred_element_type=F32)


def _dot_tn(a, b):
    return lax.dot_general(a, b, (((0,), (0,)), ((), ())), preferred_element_type=F32)


def _sigmoid(x):
    return 1.0 / (1.0 + jnp.exp(-x))


def _silu(x):
    return x * _sigmoid(x)


def _log_sigmoid(x):
    return jnp.minimum(x, 0.0) - jnp.log(1.0 + jnp.exp(-jnp.abs(x)))


def _gelu_tanh(x):
    c = np.float32(np.sqrt(2.0 / np.pi))
    return 0.5 * x * (1.0 + jnp.tanh(c * (x + 0.044715 * (x * x * x))))


def _rms_mod(x, g, sc, sh):
    ms = jnp.mean(x * x, axis=-1, keepdims=True)
    return (x * lax.rsqrt(ms + EPS)) * (g * (1.0 + sc)) + sh


def _pair_heads(f, b):
    return jnp.concatenate([p for hh in range(H)
                            for p in (f[:, hh * DK:(hh + 1) * DK], b[:, hh * DK:(hh + 1) * DK])], axis=1)


class _Tiles:
    def __init__(self, n_ctx, n_smp, tiles_per_smp_seq, lead=0):
        self.n_ctx, self.n_smp, self.tps, self.lead = n_ctx, n_smp, tiles_per_smp_seq, lead
        self.steps = lead + n_ctx + n_smp

    def tile(self, s):
        return jnp.maximum(s - self.lead, 0)

    def ctx(self, s):
        return jnp.minimum(self.tile(s), self.n_ctx - 1)

    def smp(self, s):
        return jnp.maximum(self.tile(s) - self.n_ctx, 0)

    def cond_row(self, s):
        return jnp.where(self.tile(s) < self.n_ctx, 0, SUB + self.smp(s) // self.tps)

    def specs(self, width, dual):
        if dual:
            return [pl.BlockSpec((TM, width), lambda s: (self.ctx(s), 0)),
                    pl.BlockSpec((TM, width), lambda s: (self.smp(s), 0))]
        return [pl.BlockSpec((TM, width), lambda s: (self.tile(s), 0))]

    def mod_spec(self):
        return pl.BlockSpec((1, SUB, N_MOD * D), lambda s: (0, self.cond_row(s) // SUB, 0))

    def mod_vecs(self, s, mod_ref):
        row = mod_ref[0, pl.ds(self.cond_row(s) % SUB, 1), :]
        return [row[:, i * D:(i + 1) * D] for i in range(N_MOD)]

    def whole(self, shape, layer=None):
        if layer is None:
            return pl.BlockSpec(shape, lambda s: tuple(0 for _ in shape))
        return pl.BlockSpec((1,) + shape[1:], lambda s: (layer,) + tuple(0 for _ in shape[1:]))

    def weight(self, shape):
        return pl.BlockSpec(shape, lambda s: (0, 0), pipeline_mode=pl.Buffered(1))


def _mod_kernel(cx_ref, c_ref, w_ref, b_ref, o_ref):
    w = w_ref[0].astype(BF)
    ax = jnp.broadcast_to(_silu(cx_ref[...]), (SUB, D)).astype(BF)
    o_ref[0, 0:SUB, :] = _dot(ax, w) + b_ref[0]
    o_ref[0, SUB:, :] = _dot(_silu(c_ref[...]).astype(BF), w) + b_ref[0]


def _mod_specs(c, layer, steps):
    tn = N_MOD * D // steps
    assert tn % LANE == 0
    return ([pl.BlockSpec((1, D), lambda j: (0, 0)),
             pl.BlockSpec(c.shape, lambda j: (0, 0)),
             pl.BlockSpec((1, D, tn), lambda j: (layer, 0, j)),
             pl.BlockSpec((1, 1, tn), lambda j: (layer, 0, j))],
            pl.BlockSpec((1, SUB + c.shape[0], tn), lambda j: (0, 0, j)),
            jax.ShapeDtypeStruct((1, SUB + c.shape[0], N_MOD * D), F32))


def _modulation(c_ctx, c, ada_w, ada_b, layer):
    steps = N_MOD
    in_specs, out_spec, out_shape = _mod_specs(c, layer, steps)
    return pl.pallas_call(
        _mod_kernel,
        grid=(steps,),
        in_specs=in_specs,
        out_specs=out_spec,
        out_shape=out_shape,
        compiler_params=pltpu.CompilerParams(
            dimension_semantics=("arbitrary",), vmem_limit_bytes=VMEM_LIMIT),
        name="modulation",
    )(c_ctx, c, ada_w, ada_b)


def _ev_in_kernel(tl, xc_ref, xs_ref, mod_ref, ng_ref, wc_ref, w2f_ref, w2b_ref, b2f_ref, b2b_ref,
                  dft_ref, q_ref, k_ref, la_ref, v_ref, g_ref, xcs_ref,
                  wqk_ref, wvg_ref, wlr_ref, wfin_ref, w2_ref):
    s = pl.program_id(0)
    c0 = H * DK

    @pl.when(s == 0)
    def _gate_weights():
        z = jnp.zeros((GLA_LOWRANK, c0), F32)
        w2_ref[...] = jnp.zeros_like(w2_ref)
        w2_ref[0:GLA_LOWRANK, :] = _pair_heads(w2f_ref[0], z).astype(BF)
        w2_ref[GLA_LOWRANK:2 * GLA_LOWRANK, :] = _pair_heads(z, w2b_ref[0]).astype(BF)

    @pl.when(s < tl.lead)
    def _cast_weights():
        w = wc_ref[0]
        r0 = pl.multiple_of(s * LANE, LANE)
        rows = pl.ds(r0, LANE)

        def tr(lo, hi, pad):
            piece = w[lo:hi, :]
            if pad:
                piece = jnp.concatenate([piece, jnp.zeros((pad, LANE), F32)], axis=0)
            return piece.T.astype(BF)

        wqk_ref[rows, :] = tr(0, EV_V0, 0)
        wvg_ref[rows, :] = tr(EV_V0, EV_LR0, 0)
        wlr_ref[rows, :] = tr(EV_LR0, EV_FIN0, LRP - 2 * GLA_LOWRANK)
        wfin_ref[rows, :] = tr(EV_FIN0, EV_IN, 0)

    @pl.when(s >= tl.lead)
    def _tile():
        x = jnp.where(s - tl.lead < tl.n_ctx, xc_ref[...], xs_ref[...])
        mod = tl.mod_vecs(s, mod_ref)
        h = _rms_mod(x, ng_ref[0], mod[1], mod[0]).astype(BF)

        lr = _dot(h, wlr_ref[...]).astype(BF)
        b2 = _pair_heads(b2f_ref[...], b2b_ref[...])
        nv = H * DV // PG
        for j in range(2 * nv):
            cs = slice(j * PG, (j + 1) * PG)
            if j < 2 * c0 // PG:
                pre = _dot(lr, w2_ref[:, cs]) + b2[:, cs]
                la_ref[:, cs] = _log_sigmoid(pre) * np.float32(1.0 / GLA_TAU)
            out_ref, oc = (v_ref, j) if j < nv else (g_ref, j - nv)
            out_ref[:, oc * PG:(oc + 1) * PG] = _dot(h, wvg_ref[:, cs]).astype(BF)
        q_ref[...] = (_dot(h, wqk_ref[:, 0:c0]) * np.float32(DK ** -0.5)).astype(BF)
        k_ref[...] = _dot(h, wqk_ref[:, c0:2 * c0]).astype(BF)
        fin = _dot(h, wfin_ref[...]).astype(BF)
        xcs_ref[...] = _dot(fin, dft_ref[...]).astype(BF)


def _ev_in(tl, xc, xs, mods, ngm, ev_w_in_t, w2f, w2b, b2f, b2b, dftc):
    c0 = H * DK
    assert tl.lead == D // LANE and (H * DV) % PG == 0 and 2 * DK == PG
    ntok = (tl.n_ctx + tl.n_smp) * TM
    widths = [(c0, BF), (c0, BF), (2 * c0, F32), (H * DV, BF), (H * DV, BF), (2 * FW, BF)]
    return pl.pallas_call(
        functools.partial(_ev_in_kernel, tl),
        grid=(tl.steps,),
        in_specs=tl.specs(D, True) + [
            tl.mod_spec(), tl.whole(ngm.shape, 0),
            pl.BlockSpec((1, EV_IN, LANE), lambda s: (0, 0, jnp.minimum(s, tl.lead - 1))),
            tl.whole(w2f.shape), tl.whole(w2b.shape), tl.whole(b2f.shape), tl.whole(b2b.shape),
            tl.whole(dftc.shape)],
        out_specs=[tl.specs(w, False)[0] for w, _ in widths],
        out_shape=[jax.ShapeDtypeStruct((ntok, w), dt) for w, dt in widths],
        scratch_shapes=[pltpu.VMEM((D, 2 * c0), BF), pltpu.VMEM((D, 2 * H * DV), BF),
                        pltpu.VMEM((D, LRP), BF), pltpu.VMEM((D, FW), BF),
                        pltpu.VMEM((LRP, 2 * c0), BF)],
        compiler_params=pltpu.CompilerParams(
            dimension_semantics=("arbitrary",), vmem_limit_bytes=VMEM_LIMIT),
        name="ev_in",
    )(xc, xs, mods, ngm, ev_w_in_t, w2f, w2b, b2f, b2b, dftc)


def _chunk_scan(x, pos, reverse):
    s = 1
    while s < CH:
        if reverse:
            y = pltpu.roll(x, CH - s, axis=0)
            x = x + jnp.where(pos < CH - s, y, 0.0)
        else:
            y = pltpu.roll(x, s, axis=0)
            x = x + jnp.where(pos >= s, y, 0.0)
        s *= 2
    return x


def _gla_kernel(l, nseq, has_state, n_cast, with_mod, *refs):
    q_ref, k_ref, la_ref, v_ref, g_ref, xcs_ref, gn_ref, cl_ref, sl_ref = refs[:9]
    refs = refs[9:]
    if has_state:
        s0f_ref, s0b_ref = refs[:2]
        refs = refs[2:]
    cast_src, refs = refs[:n_cast], refs[n_cast:]
    if with_mod:
        mod_in, refs = refs[:4], refs[4:]
    mix_ref, refs = refs[0], refs[1:]
    if not has_state:
        sf_ref, sb_ref = refs[:2]
        refs = refs[2:]
    cast_dst, refs = refs[:n_cast], refs[n_cast:]
    if with_mod:
        _mod_kernel(*mod_in, refs[0])
        refs = refs[1:]
    bc_ref, ke_ref, dec_ref, u_ref, sp_ref, st_ref = refs

    for src, dst in zip(cast_src, cast_dst):
        dst[...] = src[0].astype(BF)

    nchunks = l // CH
    nall = nseq * nchunks
    unroll = min(GLA_UNROLL, nall)
    pos = lax.broadcasted_iota(jnp.int32, (CH, DK), 0)
    row = lax.broadcasted_iota(jnp.int32, (CH, CH), 0)
    col = lax.broadcasted_iota(jnp.int32, (CH, CH), 1)
    gn = gn_ref[...]

    for hh in range(H):
        ks = slice(hh * DK, (hh + 1) * DK)
        vs = slice(hh * DV, (hh + 1) * DV)

        def inc_body(n, carry):
            r0 = pl.multiple_of(n * CH, CH)
            la = la_ref[pl.ds(r0, CH), hh * 2 * DK:(hh + 1) * 2 * DK]
            bf = _chunk_scan(la[:, 0:DK], pos, False)
            bb = _chunk_scan(la[:, DK:2 * DK], pos, True)
            bc_ref[pl.ds(r0, CH), 0:DK] = bf
            bc_ref[pl.ds(r0, CH), DK:2 * DK] = bb
            dec_ref[n, :, 0:DK] = jnp.exp(bf[CH - 1:CH, :])
            dec_ref[n, :, DK:2 * DK] = jnp.exp(bb[0:1, :])
            kk = k_ref[pl.ds(r0, CH), ks].astype(F32)
            ke = jnp.concatenate([kk * jnp.exp(-bf), kk * jnp.exp(-bb)], axis=1).astype(BF)
            ke_ref[pl.ds(r0, CH), :] = ke
            u_ref[n] = _dot_tn(v_ref[pl.ds(r0, CH), vs], ke)
            return carry

        lax.fori_loop(0, nall, inc_body, 0, unroll=unroll)

        for sq in range(nseq):
            if has_state:
                st_ref[:, 0:DK] = s0f_ref[sq, 0, hh]
                st_ref[:, DK:2 * DK] = s0b_ref[sq, 0, hh]
            else:
                st_ref[...] = jnp.zeros_like(st_ref)

            def scan_body(i, carry, sq=sq):
                nf = sq * nchunks + i
                nb = sq * nchunks + nchunks - 1 - i
                sf = st_ref[:, 0:DK]
                sb = st_ref[:, DK:2 * DK]
                sp_ref[nf, :, 0:DK] = sf.astype(BF)
                sp_ref[nb, :, DK:2 * DK] = sb.astype(BF)
                st_ref[:, 0:DK] = dec_ref[nf, :, 0:DK] * (sf + u_ref[nf, :, 0:DK])
                st_ref[:, DK:2 * DK] = dec_ref[nb, :, DK:2 * DK] * (sb + u_ref[nb, :, DK:2 * DK])
                return carry

            lax.fori_loop(0, nchunks, scan_body, 0)
            if not has_state:
                sf_ref[sq, 0, hh] = st_ref[:, 0:DK]
                sb_ref[sq, 0, hh] = st_ref[:, DK:2 * DK]

        def out_body(n, carry):
            r0 = pl.multiple_of(n * CH, CH)
            bc = bc_ref[pl.ds(r0, CH), :]
            qq = q_ref[pl.ds(r0, CH), ks].astype(F32)
            qe = jnp.concatenate([qq * jnp.exp(bc[:, 0:DK]), qq * jnp.exp(bc[:, DK:2 * DK])],
                                 axis=1).astype(BF)
            ke = ke_ref[pl.ds(r0, CH), :]
            att = (jnp.where(row >= col, _dot_nt(qe[:, 0:DK], ke[:, 0:DK]), 0.0)
                   + jnp.where(row <= col, _dot_nt(qe[:, DK:], ke[:, DK:]), 0.0)).astype(BF)
            o = _dot(att, v_ref[pl.ds(r0, CH), vs]) + _dot_nt(qe, sp_ref[n])
            ms = jnp.mean(o * o, axis=-1, keepdims=True)
            on = o * lax.rsqrt(ms + EPS) * gn
            gate = _silu(g_ref[pl.ds(r0, CH), vs].astype(F32))
            mix_ref[pl.ds(r0, CH), vs] = (on * gate).astype(BF)
            return carry

        lax.fori_loop(0, nall, out_body, 0, unroll=unroll)

    for sq in range(nseq):
        rs = slice(sq * l, (sq + 1) * l)
        fo = _dot(cl_ref[...], xcs_ref[rs, 0:FW]) + _dot(sl_ref[...], xcs_ref[rs, FW:2 * FW])
        mix_ref[rs, H * DV:H * DV + FW] = fo.astype(BF)


def _gla(b, l, nseq, tok0, q, k, la, v, g, xcs, gn, cl, sl, states, casts, mod=None):
    nchunks = l // CH
    c0 = H * DK
    rows = nseq * l
    steps = b // nseq
    assert tok0 % rows == 0 and b % nseq == 0
    seq = lambda w: pl.BlockSpec((rows, w), lambda i: (tok0 // rows + i, 0))
    const = lambda shape: pl.BlockSpec(shape, lambda i: tuple(0 for _ in shape))
    once = lambda shape: pl.BlockSpec(shape, lambda i: (0, 0), pipeline_mode=pl.Buffered(1))
    st_spec = pl.BlockSpec((nseq, 1, H, DV, DK), lambda i: (i, 0, 0, 0, 0))
    in_specs = [seq(c0), seq(c0), seq(2 * c0), seq(H * DV), seq(H * DV), seq(2 * FW),
                const(gn.shape), once((l, l)), once((l, l))]
    out_specs = [pl.BlockSpec((rows, MIX0), lambda i: (i, 0))]
    out_shape = [jax.ShapeDtypeStruct((b * l, MIX0), BF)]
    args = [q, k, la, v, g, xcs, gn, cl, sl]
    if states is not None:
        in_specs += [st_spec, st_spec]
        args += list(states)
    else:
        out_specs += [st_spec, st_spec]
        out_shape += [jax.ShapeDtypeStruct((b, 1, H, DV, DK), F32)] * 2
    for w, layer in casts:
        _, r, c = w.shape
        assert r % steps == 0
        in_specs.append(pl.BlockSpec((1, r // steps, c), lambda i, layer=layer: (layer, i, 0)))
        args.append(w)
        out_specs.append(pl.BlockSpec((r // steps, c), lambda i: (i, 0)))
        out_shape.append(jax.ShapeDtypeStruct((r, c), BF))
    if mod is not None:
        mod_in, mod_out, mod_shape = _mod_specs(mod[1], mod[4], steps)
        in_specs += mod_in
        args += list(mod[:4])
        out_specs.append(mod_out)
        out_shape.append(mod_shape)
    return pl.pallas_call(
        functools.partial(_gla_kernel, l, nseq, states is not None, len(casts), mod is not None),
        grid=(steps,),
        in_specs=in_specs,
        out_specs=out_specs,
        out_shape=out_shape,
        scratch_shapes=[pltpu.VMEM((rows, 2 * DK), F32),
                        pltpu.VMEM((rows, 2 * DK), BF),
                        pltpu.VMEM((nseq * nchunks, 1, 2 * DK), F32),
                        pltpu.VMEM((nseq * nchunks, DV, 2 * DK), F32),
                        pltpu.VMEM((nseq * nchunks, DV, 2 * DK), BF),
                        pltpu.VMEM((DV, 2 * DK), F32)],
        compiler_params=pltpu.CompilerParams(
            dimension_semantics=("arbitrary",), vmem_limit_bytes=VMEM_LIMIT),
        name="gla_fnet",
    )(*args)


def _ffn(x1, mod, ng, w1_ref, w2_ref):
    h = _rms_mod(x1, ng, mod[4], mod[3]).astype(BF)
    wd = D_FF // FF_SPLIT
    out = None
    for c in range(FF_SPLIT):
        hid = jnp.maximum(_dot(h, w1_ref[:, c * wd:(c + 1) * wd]), 0.0)
        part = _dot((hid * hid).astype(BF), w2_ref[c * wd:(c + 1) * wd, :])
        out = part if out is None else out + part
    return x1 + mod[5] * out


def _ev_out_kernel(tl, n_cast, xc_ref, xs_ref, mc_ref, ms_ref, mod_ref, ng_ref, wo_ref, w1_ref,
                   w2_ref, *refs):
    cast_src, mod_in = refs[:n_cast], refs[n_cast:n_cast + 4]
    o_ref = refs[n_cast + 4]
    cast_dst, mod_out = refs[n_cast + 5:2 * n_cast + 5], refs[2 * n_cast + 5]
    s = pl.program_id(0)

    @pl.when(s < CAST_STEPS)
    def _cast_weights():
        for src, dst in zip(cast_src, cast_dst):
            dst[...] = src[0].astype(BF)

    _mod_kernel(*mod_in, mod_out)

    is_ctx = s < tl.n_ctx
    mod = tl.mod_vecs(s, mod_ref)
    x = jnp.where(is_ctx, xc_ref[...], xs_ref[...])
    mix = jnp.where(is_ctx, mc_ref[...], ms_ref[...])
    x1 = x + mod[2] * _dot(mix, wo_ref[...])
    o_ref[...] = _ffn(x1, mod, ng_ref[0], w1_ref, w2_ref)


def _ev_out(tl, xc, xs, mix_c, mix_s, mods, ngf, wo, w1, w2, layer, casts, mod):
    ntok = (tl.n_ctx + tl.n_smp) * TM
    assert tl.steps >= CAST_STEPS
    chunk = lambda s: jnp.minimum(s, CAST_STEPS - 1)
    cast_in, cast_out, cast_shape = [], [], []
    for w, wl in casts:
        _, r, c = w.shape
        assert r % CAST_STEPS == 0
        cast_in.append(pl.BlockSpec((1, r // CAST_STEPS, c), lambda s, wl=wl: (wl, chunk(s), 0)))
        cast_out.append(pl.BlockSpec((r // CAST_STEPS, c), lambda s: (chunk(s), 0)))
        cast_shape.append(jax.ShapeDtypeStruct((r, c), BF))
    mod_in, mod_out, mod_shape = _mod_specs(mod[1], mod[4], tl.steps)
    return pl.pallas_call(
        functools.partial(_ev_out_kernel, tl, len(casts)),
        grid=(tl.steps,),
        in_specs=tl.specs(D, True) + tl.specs(MIX0, True) + [
            tl.mod_spec(), tl.whole(ngf.shape, layer),
            tl.weight(wo.shape), tl.weight(w1.shape), tl.weight(w2.shape)] + cast_in + mod_in,
        out_specs=[tl.specs(D, False)[0]] + cast_out + [mod_out],
        out_shape=[jax.ShapeDtypeStruct((ntok, D), F32)] + cast_shape + [mod_shape],
        compiler_params=pltpu.CompilerParams(
            dimension_semantics=("arbitrary",), vmem_limit_bytes=VMEM_LIMIT),
        name="ev_out_ffn",
    )(xc, xs, mix_c, mix_s, mods, ngf, wo, w1, w2, *[w for w, _ in casts], *mod[:4])


def _odd_kernel(tl, ctx_rows, smp_rows, x_ref, mod_ref, ngm_ref, ngf_ref, gfin_ref, ws_ref, gb_ref,
                cw_ref, wi_ref, wo_ref, w1_ref, w2_ref, yc_ref, ys_ref):
    s = pl.program_id(0)
    is_ctx = s < tl.n_ctx
    x = x_ref[...]
    mod = tl.mod_vecs(s, mod_ref)
    h = _rms_mod(x, ngm_ref[0], mod[1], mod[0]).astype(BF)
    v = _gelu_tanh(_dot(h, wi_ref[:, GW:2 * GW])).astype(BF)
    u = _gelu_tanh(_dot(h, wi_ref[:, 0:GW]))
    z = (_dot(h, wi_ref[:, 2 * GW + CW:2 * GW + 2 * CW])
         * _dot(h, wi_ref[:, 2 * GW + 2 * CW:2 * GW + 3 * CW]))
    gate_b = _dot(h, wi_ref[:, 2 * GW:2 * GW + CW])

    sp_rows = []
    for c in range(TM // GCH):
        cols = []
        for gi in range(GW // GCH):
            vg = v[c * GCH:(c + 1) * GCH, gi * GCH:(gi + 1) * GCH]
            cols.append(_dot(ws_ref[0, gi].astype(BF), vg))
        sp_rows.append(jnp.concatenate(cols, axis=1) + gb_ref[...])
    out_c = u * jnp.concatenate(sp_rows, axis=0)

    last = jnp.where(is_ctx, ctx_rows - 1, smp_rows - 1)
    pos = lax.broadcasted_iota(jnp.int32, (TM, CW), 0) & last
    zl = jnp.where(pos >= 1, pltpu.roll(z, 1, axis=0), 0.0)
    zr = jnp.where(pos < last, pltpu.roll(z, TM - 1, axis=0), 0.0)
    cw = cw_ref[0]
    out_d = gate_b * (zl * cw[0:1] + z * cw[1:2] + zr * cw[2:3])

    mix = jnp.concatenate([out_c, out_d], axis=1).astype(BF)
    x1 = x + mod[2] * _dot(mix, wo_ref[...])
    x2 = _ffn(x1, mod, ngf_ref[0], w1_ref, w2_ref)
    ms = jnp.mean(x2 * x2, axis=-1, keepdims=True)
    y = x2 * lax.rsqrt(ms + EPS) * gfin_ref[...]

    @pl.when(is_ctx)
    def _store_ctx():
        yc_ref[...] = y

    @pl.when(jnp.logical_not(is_ctx))
    def _store_smp():
        ys_ref[...] = y


def _odd(tl, ctx_rows, smp_rows, x, mods, ngm, ngf, gfin, gmlp_ws, gbias, conv_w, wi, wo, w1, w2,
         layer):
    for r in (ctx_rows, smp_rows):
        assert TM % r == 0 and r & (r - 1) == 0
    assert TM % GCH == 0
    return pl.pallas_call(
        functools.partial(_odd_kernel, tl, ctx_rows, smp_rows),
        grid=(tl.steps,),
        in_specs=tl.specs(D, False) + [
            tl.mod_spec(), tl.whole(ngm.shape, layer), tl.whole(ngf.shape, layer),
            tl.whole(gfin.shape), tl.whole(gmlp_ws.shape), tl.whole(gbias.shape),
            tl.whole(conv_w.shape),
            tl.weight(wi.shape), tl.weight(wo.shape), tl.weight(w1.shape), tl.weight(w2.shape)],
        out_specs=tl.specs(D, True),
        out_shape=[jax.ShapeDtypeStruct((tl.n_ctx * TM, D), F32),
                   jax.ShapeDtypeStruct((tl.n_smp * TM, D), F32)],
        compiler_params=pltpu.CompilerParams(
            dimension_semantics=("arbitrary",), vmem_limit_bytes=VMEM_LIMIT),
        name="odd_ffn_final",
    )(x, mods, ngm, ngf, gfin, gmlp_ws, gbias, conv_w, wi, wo, w1, w2)


def _dft_pos(l):
    n = np.arange(l)
    ang = 2.0 * np.pi * ((n[:, None] * n[None, :]) % l) / l
    to_bf = lambda a: jnp.asarray(a, F32).astype(BF)
    return to_bf(np.cos(ang) / np.sqrt(l)), to_bf(-np.sin(ang) / np.sqrt(l))


def _dft_chan():
    m = np.arange(FG)
    angc = 2.0 * np.pi * ((m[:, None] * m[None, :]) % FG) / FG
    eye = np.eye(FW // FG)
    cc = np.kron(eye, np.cos(angc)) / np.sqrt(FG)
    sc = np.kron(eye, np.sin(angc)) / np.sqrt(FG)
    return jnp.asarray(np.concatenate([cc, sc], axis=1), F32).astype(BF)


def kernel(x_prompt, x_sample, state_gla_fwd, state_gla_bwd, c, c_ctx, ada_w, ada_b, norm_mix_g,
           norm_ffn_g, ffn_w1, ffn_w2, ev_w_in, ev_w_out, gla_w2_f, gla_b2_f, gla_w2_b, gla_b2_b,
           gla_norm_g, od_w_in, od_w_out, gmlp_ws, gmlp_b, conv_w, final_norm_g):
    b_ctx, l_ctx, _ = x_prompt.shape
    b_smp, l_smp, _ = x_sample.shape
    depth = ada_w.shape[0]
    assert TM % l_ctx == 0 and l_smp % TM == 0 and b_smp % SUB == 0
    n_ctx, n_smp, tps = b_ctx * l_ctx // TM, b_smp * l_smp // TM, l_smp // TM
    tl = _Tiles(n_ctx, n_smp, tps)

    mod_args = (c_ctx.reshape(1, D), c, ada_w, ada_b.reshape(depth, 1, N_MOD * D))
    mods0 = _modulation(*mod_args, 0)
    ngm = norm_mix_g.reshape(depth, 1, D)
    ngf = norm_ffn_g.reshape(depth, 1, D)
    gbias = jnp.repeat(gmlp_b[0].T, GCH, axis=1)
    xc = x_prompt.reshape(b_ctx * l_ctx, D)
    xs = x_sample.reshape(b_smp * l_smp, D)

    q, k, la, v, g, xcs = _ev_in(_Tiles(n_ctx, n_smp, tps, lead=D // LANE), xc, xs, mods0, ngm,
                                 jnp.swapaxes(ev_w_in, 1, 2), gla_w2_f, gla_w2_b, gla_b2_f,
                                 gla_b2_b, _dft_chan())
    mix_c, sf, sb, wo0, w1_0, w2_0 = _gla(
        b_ctx, l_ctx, CTX_SEQS, 0, q, k, la, v, g, xcs, gla_norm_g, *_dft_pos(l_ctx), None,
        [(ev_w_out, 0), (ffn_w1, 0), (ffn_w2, 0)])
    (mix_s,) = _gla(
        b_smp, l_smp, 1, b_ctx * l_ctx, q, k, la, v, g, xcs, gla_norm_g, *_dft_pos(l_smp),
        (jnp.swapaxes(state_gla_fwd, -1, -2), jnp.swapaxes(state_gla_bwd, -1, -2)), [])
    x1, wi1, wo1, w1_1, w2_1, mods1 = _ev_out(
        tl, xc, xs, mix_c, mix_s, mods0, ngf, wo0, w1_0, w2_0, 0,
        [(od_w_in, 0), (od_w_out, 0), (ffn_w1, 1), (ffn_w2, 1)], mod_args + (1,))

    yc, ys = _odd(tl, l_ctx, GRID_W, x1, mods1, ngm, ngf, final_norm_g.reshape(1, D), gmlp_ws, gbias,
                  conv_w, wi1, wo1, w1_1, w2_1, 1)
    return (yc.reshape(b_ctx, l_ctx, D), ys.reshape(b_smp, l_smp, D),
            jnp.swapaxes(sf, -1, -2), jnp.swapaxes(sb, -1, -2))
```

```python
import functools

import jax
import jax.numpy as jnp
import numpy as np
from jax import lax
from jax.experimental import pallas as pl
from jax.experimental.pallas import tpu as pltpu

D = 1024
D_FF = 4 * D
EPS = 1e-6
N_MOD = 6
GRID_W = 64

H = 4
DK = 128
DV = 192
GLA_LOWRANK = 16
GLA_TAU = 16.0
CH = 64
FW = 256
FG = 64
GW = 512
GCH = 128
CW = 512

EV_V0 = 2 * H * DK
EV_G0 = EV_V0 + H * DV
EV_LR0 = EV_G0 + H * DV
EV_FIN0 = EV_LR0 + 2 * GLA_LOWRANK
EV_IN = EV_FIN0 + FW

TM = 512
FF_SPLIT = 2
LRP = 128
PG = 256
MIX0 = H * DV + FW
OD_COLS = 2 * GW + 3 * CW
GLA_UNROLL = 16
CTX_SEQS = 2
SUB = 8
LANE = 128

VMEM_LIMIT = 62 * 1024 * 1024

BF = jnp.bfloat16
F32 = jnp.float32


def _dot(a, b):
    return jnp.dot(a, b, preferred_element_type=F32)


def _dot_nt(a, b):
    return lax.dot_general(a, b, (((1,), (1,)), ((), ())), preferred_element_type=F32)


def _dot_tn(a, b):
    return lax.dot_general(a, b, (((0,), (0,)), ((), ())), preferred_element_type=F32)


def _sigmoid(x):
    return 1.0 / (1.0 + jnp.exp(-x))


def _silu(x):
    return x * _sigmoid(x)


def _log_sigmoid(x):
    return jnp.minimum(x, 0.0) - jnp.log(1.0 + jnp.exp(-jnp.abs(x)))


def _gelu_tanh(x):
    c = np.float32(np.sqrt(2.0 / np.pi))
    return 0.5 * x * (1.0 + jnp.tanh(c * (x + 0.044715 * (x * x * x))))


def _rms_mod(x, g, sc, sh):
    ms = jnp.mean(x * x, axis=-1, keepdims=True)
    return (x * lax.rsqrt(ms + EPS)) * (g * (1.0 + sc)) + sh


def _pair_heads(f, b):
    return jnp.concatenate([p for hh in range(H)
                            for p in (f[:, hh * DK:(hh + 1) * DK], b[:, hh * DK:(hh + 1) * DK])], axis=1)


class _Tiles:
    def __init__(self, n_ctx, n_smp, tiles_per_smp_seq, lead=0):
        self.n_ctx, self.n_smp, self.tps, self.lead = n_ctx, n_smp, tiles_per_smp_seq, lead
        self.steps = lead + n_ctx + n_smp

    def tile(self, s):
        return jnp.maximum(s - self.lead, 0)

    def ctx(self, s):
        return jnp.minimum(self.tile(s), self.n_ctx - 1)

    def smp(self, s):
        return jnp.maximum(self.tile(s) - self.n_ctx, 0)

    def cond_row(self, s):
        return jnp.where(self.tile(s) < self.n_ctx, 0, SUB + self.smp(s) // self.tps)

    def specs(self, width, dual):
        if dual:
            return [pl.BlockSpec((TM, width), lambda s: (self.ctx(s), 0)),
                    pl.BlockSpec((TM, width), lambda s: (self.smp(s), 0))]
        return [pl.BlockSpec((TM, width), lambda s: (self.tile(s), 0))]

    def mod_spec(self):
        return pl.BlockSpec((1, SUB, N_MOD * D), lambda s: (0, self.cond_row(s) // SUB, 0))

    def mod_vecs(self, s, mod_ref):
        row = mod_ref[0, pl.ds(self.cond_row(s) % SUB, 1), :]
        return [row[:, i * D:(i + 1) * D] for i in range(N_MOD)]

    def whole(self, shape, layer=None):
        if layer is None:
            return pl.BlockSpec(shape, lambda s: tuple(0 for _ in shape))
        return pl.BlockSpec((1,) + shape[1:], lambda s: (layer,) + tuple(0 for _ in shape[1:]))

    def weight(self, shape):
        return pl.BlockSpec(shape, lambda s: (0, 0), pipeline_mode=pl.Buffered(1))


def _mod_kernel(cx_ref, c_ref, w_ref, b_ref, o_ref):
    w = w_ref[0].astype(BF)
    ax = jnp.broadcast_to(_silu(cx_ref[...]), (SUB, D)).astype(BF)
    o_ref[0, 0:SUB, :] = _dot(ax, w) + b_ref[0]
    o_ref[0, SUB:, :] = _dot(_silu(c_ref[...]).astype(BF), w) + b_ref[0]


def _mod_specs(c, layer, steps, block=lambda j: j):
    tn = N_MOD * D // steps
    assert tn % LANE == 0
    return ([pl.BlockSpec((1, D), lambda j: (0, 0)),
             pl.BlockSpec(c.shape, lambda j: (0, 0)),
             pl.BlockSpec((1, D, tn), lambda j: (layer, 0, block(j))),
             pl.BlockSpec((1, 1, tn), lambda j: (layer, 0, block(j)))],
            pl.BlockSpec((1, SUB + c.shape[0], tn), lambda j: (0, 0, block(j))),
            jax.ShapeDtypeStruct((1, SUB + c.shape[0], N_MOD * D), F32))


def _ev_in_kernel(tl, xc_ref, xs_ref, cx_ref, c_ref, aw_ref, ab_ref, ng_ref, wc_ref, w2f_ref,
                  w2b_ref, b2f_ref, b2b_ref, dft_ref,
                  q_ref, k_ref, la_ref, v_ref, g_ref, xcs_ref, mods_ref,
                  wqk_ref, wvg_ref, wlr_ref, wfin_ref, w2_ref, mod_s):
    s = pl.program_id(0)
    c0 = H * DK

    @pl.when(s < N_MOD)
    def _modulation():
        _mod_kernel(cx_ref, c_ref, aw_ref, ab_ref, mods_ref)
        mod_s[s] = mods_ref[0]

    @pl.when(s == 0)
    def _gate_weights():
        z = jnp.zeros((GLA_LOWRANK, c0), F32)
        w2_ref[...] = jnp.zeros_like(w2_ref)
        w2_ref[0:GLA_LOWRANK, :] = _pair_heads(w2f_ref[0], z).astype(BF)
        w2_ref[GLA_LOWRANK:2 * GLA_LOWRANK, :] = _pair_heads(z, w2b_ref[0]).astype(BF)

    @pl.when(s < tl.lead)
    def _cast_weights():
        w = wc_ref[0]
        r0 = pl.multiple_of(s * LANE, LANE)
        rows = pl.ds(r0, LANE)

        def tr(lo, hi, pad):
            piece = w[lo:hi, :]
            if pad:
                piece = jnp.concatenate([piece, jnp.zeros((pad, LANE), F32)], axis=0)
            return piece.T.astype(BF)

        wqk_ref[rows, :] = tr(0, EV_V0, 0)
        wvg_ref[rows, :] = tr(EV_V0, EV_LR0, 0)
        wlr_ref[rows, :] = tr(EV_LR0, EV_FIN0, LRP - 2 * GLA_LOWRANK)
        wfin_ref[rows, :] = tr(EV_FIN0, EV_IN, 0)

    @pl.when(s >= tl.lead)
    def _tile():
        x = jnp.where(s - tl.lead < tl.n_ctx, xc_ref[...], xs_ref[...])
        crow = tl.cond_row(s)
        h = _rms_mod(x, ng_ref[0], mod_s[1, pl.ds(crow, 1), :], mod_s[0, pl.ds(crow, 1), :]).astype(BF)

        lr = _dot(h, wlr_ref[...]).astype(BF)
        b2 = _pair_heads(b2f_ref[...], b2b_ref[...])
        nv = H * DV // PG
        for j in range(2 * nv):
            cs = slice(j * PG, (j + 1) * PG)
            if j < 2 * c0 // PG:
                pre = _dot(lr, w2_ref[:, cs]) + b2[:, cs]
                la_ref[:, cs] = _log_sigmoid(pre) * np.float32(1.0 / GLA_TAU)
            out_ref, oc = (v_ref, j) if j < nv else (g_ref, j - nv)
            out_ref[:, oc * PG:(oc + 1) * PG] = _dot(h, wvg_ref[:, cs]).astype(BF)
        q_ref[...] = (_dot(h, wqk_ref[:, 0:c0]) * np.float32(DK ** -0.5)).astype(BF)
        k_ref[...] = _dot(h, wqk_ref[:, c0:2 * c0]).astype(BF)
        fin = _dot(h, wfin_ref[...]).astype(BF)
        xcs_ref[...] = _dot(fin, dft_ref[...]).astype(BF)


def _ev_in(tl, xc, xs, mod, ngm, ev_w_in_t, w2f, w2b, b2f, b2b, dftc):
    c0 = H * DK
    assert tl.lead == D // LANE >= N_MOD and (H * DV) % PG == 0 and 2 * DK == PG
    ntok = (tl.n_ctx + tl.n_smp) * TM
    widths = [(c0, BF), (c0, BF), (2 * c0, F32), (H * DV, BF), (H * DV, BF), (2 * FW, BF)]
    mod_in, mod_out, mod_shape = _mod_specs(mod[1], mod[4], N_MOD, lambda s: jnp.minimum(s, N_MOD - 1))
    return pl.pallas_call(
        functools.partial(_ev_in_kernel, tl),
        grid=(tl.steps,),
        in_specs=tl.specs(D, True) + mod_in + [
            tl.whole(ngm.shape, 0),
            pl.BlockSpec((1, EV_IN, LANE), lambda s: (0, 0, jnp.minimum(s, tl.lead - 1))),
            tl.whole(w2f.shape), tl.whole(w2b.shape), tl.whole(b2f.shape), tl.whole(b2b.shape),
            tl.whole(dftc.shape)],
        out_specs=[tl.specs(w, False)[0] for w, _ in widths] + [mod_out],
        out_shape=[jax.ShapeDtypeStruct((ntok, w), dt) for w, dt in widths] + [mod_shape],
        scratch_shapes=[pltpu.VMEM((D, 2 * c0), BF), pltpu.VMEM((D, 2 * H * DV), BF),
                        pltpu.VMEM((D, LRP), BF), pltpu.VMEM((D, FW), BF),
                        pltpu.VMEM((LRP, 2 * c0), BF),
                        pltpu.VMEM((N_MOD, SUB + mod[1].shape[0], D), F32)],
        compiler_params=pltpu.CompilerParams(
            dimension_semantics=("arbitrary",), vmem_limit_bytes=VMEM_LIMIT),
        name="ev_in",
    )(xc, xs, *mod[:4], ngm, ev_w_in_t, w2f, w2b, b2f, b2b, dftc)


def _chunk_scan(x, pos, reverse):
    s = 1
    while s < CH:
        if reverse:
            y = pltpu.roll(x, CH - s, axis=0)
            x = x + jnp.where(pos < CH - s, y, 0.0)
        else:
            y = pltpu.roll(x, s, axis=0)
            x = x + jnp.where(pos >= s, y, 0.0)
        s *= 2
    return x


def _gla_kernel(l, nseq, has_state, n_cast, with_mod, *refs):
    q_ref, k_ref, la_ref, v_ref, g_ref, xcs_ref, gn_ref, cl_ref, sl_ref = refs[:9]
    refs = refs[9:]
    if has_state:
        s0f_ref, s0b_ref = refs[:2]
        refs = refs[2:]
    cast_src, refs = refs[:n_cast], refs[n_cast:]
    if with_mod:
        mod_in, refs = refs[:4], refs[4:]
    mix_ref, refs = refs[0], refs[1:]
    if not has_state:
        sf_ref, sb_ref = refs[:2]
        refs = refs[2:]
    cast_dst, refs = refs[:n_cast], refs[n_cast:]
    if with_mod:
        _mod_kernel(*mod_in, refs[0])
        refs = refs[1:]
    bc_ref, ke_ref, dec_ref, u_ref, sp_ref, st_ref = refs

    for src, dst in zip(cast_src, cast_dst):
        dst[...] = src[0].astype(BF)

    nchunks = l // CH
    nall = nseq * nchunks
    unroll = min(GLA_UNROLL, nall)
    pos = lax.broadcasted_iota(jnp.int32, (CH, DK), 0)
    row = lax.broadcasted_iota(jnp.int32, (CH, CH), 0)
    col = lax.broadcasted_iota(jnp.int32, (CH, CH), 1)
    gn = gn_ref[...]

    for hh in range(H):
        ks = slice(hh * DK, (hh + 1) * DK)
        vs = slice(hh * DV, (hh + 1) * DV)

        def inc_body(n, carry):
            r0 = pl.multiple_of(n * CH, CH)
            la = la_ref[pl.ds(r0, CH), hh * 2 * DK:(hh + 1) * 2 * DK]
            bf = _chunk_scan(la[:, 0:DK], pos, False)
            bb = _chunk_scan(la[:, DK:2 * DK], pos, True)
            bc_ref[pl.ds(r0, CH), 0:DK] = bf
            bc_ref[pl.ds(r0, CH), DK:2 * DK] = bb
            dec_ref[n, :, 0:DK] = jnp.exp(bf[CH - 1:CH, :])
            dec_ref[n, :, DK:2 * DK] = jnp.exp(bb[0:1, :])
            kk = k_ref[pl.ds(r0, CH), ks].astype(F32)
            ke = jnp.concatenate([kk * jnp.exp(-bf), kk * jnp.exp(-bb)], axis=1).astype(BF)
            ke_ref[pl.ds(r0, CH), :] = ke
            u_ref[n] = _dot_tn(v_ref[pl.ds(r0, CH), vs], ke)
            return carry

        lax.fori_loop(0, nall, inc_body, 0, unroll=unroll)

        for sq in range(nseq):
            if has_state:
                st_ref[:, 0:DK] = s0f_ref[sq, 0, hh]
                st_ref[:, DK:2 * DK] = s0b_ref[sq, 0, hh]
            else:
                st_ref[...] = jnp.zeros_like(st_ref)

            def scan_body(i, carry, sq=sq):
                nf = sq * nchunks + i
                nb = sq * nchunks + nchunks - 1 - i
                sf = st_ref[:, 0:DK]
                sb = st_ref[:, DK:2 * DK]
                sp_ref[nf, :, 0:DK] = sf.astype(BF)
                sp_ref[nb, :, DK:2 * DK] = sb.astype(BF)
                st_ref[:, 0:DK] = dec_ref[nf, :, 0:DK] * (sf + u_ref[nf, :, 0:DK])
                st_ref[:, DK:2 * DK] = dec_ref[nb, :, DK:2 * DK] * (sb + u_ref[nb, :, DK:2 * DK])
                return carry

            lax.fori_loop(0, nchunks, scan_body, 0)
            if not has_state:
                sf_ref[sq, 0, hh] = st_ref[:, 0:DK]
                sb_ref[sq, 0, hh] = st_ref[:, DK:2 * DK]

        def out_body(n, carry):
            r0 = pl.multiple_of(n * CH, CH)
            bc = bc_ref[pl.ds(r0, CH), :]
            qq = q_ref[pl.ds(r0, CH), ks].astype(F32)
            qe = jnp.concatenate([qq * jnp.exp(bc[:, 0:DK]), qq * jnp.exp(bc[:, DK:2 * DK])],
                                 axis=1).astype(BF)
            ke = ke_ref[pl.ds(r0, CH), :]
            att = (jnp.where(row >= col, _dot_nt(qe[:, 0:DK], ke[:, 0:DK]), 0.0)
                   + jnp.where(row <= col, _dot_nt(qe[:, DK:], ke[:, DK:]), 0.0)).astype(BF)
            o = _dot(att, v_ref[pl.ds(r0, CH), vs]) + _dot_nt(qe, sp_ref[n])
            ms = jnp.mean(o * o, axis=-1, keepdims=True)
            on = o * lax.rsqrt(ms + EPS) * gn
            gate = _silu(g_ref[pl.ds(r0, CH), vs].astype(F32))
            mix_ref[pl.ds(r0, CH), vs] = (on * gate).astype(BF)
            return carry

        lax.fori_loop(0, nall, out_body, 0, unroll=unroll)

    for sq in range(nseq):
        rs = slice(sq * l, (sq + 1) * l)
        fo = _dot(cl_ref[...], xcs_ref[rs, 0:FW]) + _dot(sl_ref[...], xcs_ref[rs, FW:2 * FW])
        mix_ref[rs, H * DV:H * DV + FW] = fo.astype(BF)


def _gla(b, l, nseq, tok0, q, k, la, v, g, xcs, gn, cl, sl, states, casts, mod=None):
    nchunks = l // CH
    c0 = H * DK
    rows = nseq * l
    steps = b // nseq
    assert tok0 % rows == 0 and b % nseq == 0
    seq = lambda w: pl.BlockSpec((rows, w), lambda i: (tok0 // rows + i, 0))
    const = lambda shape: pl.BlockSpec(shape, lambda i: tuple(0 for _ in shape))
    once = lambda shape: pl.BlockSpec(shape, lambda i: (0, 0), pipeline_mode=pl.Buffered(1))
    st_spec = pl.BlockSpec((nseq, 1, H, DV, DK), lambda i: (i, 0, 0, 0, 0))
    in_specs = [seq(c0), seq(c0), seq(2 * c0), seq(H * DV), seq(H * DV), seq(2 * FW),
                const(gn.shape), once((l, l)), once((l, l))]
    out_specs = [pl.BlockSpec((rows, MIX0), lambda i: (i, 0))]
    out_shape = [jax.ShapeDtypeStruct((b * l, MIX0), BF)]
    args = [q, k, la, v, g, xcs, gn, cl, sl]
    if states is not None:
        in_specs += [st_spec, st_spec]
        args += list(states)
    else:
        out_specs += [st_spec, st_spec]
        out_shape += [jax.ShapeDtypeStruct((b, 1, H, DV, DK), F32)] * 2
    for w, layer in casts:
        _, r, c = w.shape
        assert r % steps == 0
        in_specs.append(pl.BlockSpec((1, r // steps, c), lambda i, layer=layer: (layer, i, 0)))
        args.append(w)
        out_specs.append(pl.BlockSpec((r // steps, c), lambda i: (i, 0)))
        out_shape.append(jax.ShapeDtypeStruct((r, c), BF))
    if mod is not None:
        mod_in, mod_out, mod_shape = _mod_specs(mod[1], mod[4], steps)
        in_specs += mod_in
        args += list(mod[:4])
        out_specs.append(mod_out)
        out_shape.append(mod_shape)
    return pl.pallas_call(
        functools.partial(_gla_kernel, l, nseq, states is not None, len(casts), mod is not None),
        grid=(steps,),
        in_specs=in_specs,
        out_specs=out_specs,
        out_shape=out_shape,
        scratch_shapes=[pltpu.VMEM((rows, 2 * DK), F32),
                        pltpu.VMEM((rows, 2 * DK), BF),
                        pltpu.VMEM((nseq * nchunks, 1, 2 * DK), F32),
                        pltpu.VMEM((nseq * nchunks, DV, 2 * DK), F32),
                        pltpu.VMEM((nseq * nchunks, DV, 2 * DK), BF),
                        pltpu.VMEM((DV, 2 * DK), F32)],
        compiler_params=pltpu.CompilerParams(
            dimension_semantics=("arbitrary",), vmem_limit_bytes=VMEM_LIMIT),
        name="gla_fnet",
    )(*args)


def _ffn(x1, mod, ng, w1_ref, w2_ref):
    h = _rms_mod(x1, ng, mod[4], mod[3]).astype(BF)
    wd = D_FF // FF_SPLIT
    out = None
    for c in range(FF_SPLIT):
        hid = jnp.maximum(_dot(h, w1_ref[:, c * wd:(c + 1) * wd]), 0.0)
        part = _dot((hid * hid).astype(BF), w2_ref[c * wd:(c + 1) * wd, :])
        out = part if out is None else out + part
    return x1 + mod[5] * out


def _ev_out_kernel(tl, xc_ref, xs_ref, mc_ref, ms_ref, mod_ref, ng_ref, wo_ref, w1_ref, w2_ref,
                   o_ref):
    s = pl.program_id(0)
    is_ctx = s < tl.n_ctx
    mod = tl.mod_vecs(s, mod_ref)
    x = jnp.where(is_ctx, xc_ref[...], xs_ref[...])
    mix = jnp.where(is_ctx, mc_ref[...], ms_ref[...])
    x1 = x + mod[2] * _dot(mix, wo_ref[...])
    o_ref[...] = _ffn(x1, mod, ng_ref[0], w1_ref, w2_ref)


def _ev_out(tl, xc, xs, mix_c, mix_s, mods, ngf, wo, w1, w2, layer):
    ntok = (tl.n_ctx + tl.n_smp) * TM
    return pl.pallas_call(
        functools.partial(_ev_out_kernel, tl),
        grid=(tl.steps,),
        in_specs=tl.specs(D, True) + tl.specs(MIX0, True) + [
            tl.mod_spec(), tl.whole(ngf.shape, layer),
            tl.weight(wo.shape), tl.weight(w1.shape), tl.weight(w2.shape)],
        out_specs=tl.specs(D, False)[0],
        out_shape=jax.ShapeDtypeStruct((ntok, D), F32),
        compiler_params=pltpu.CompilerParams(
            dimension_semantics=("arbitrary",), vmem_limit_bytes=VMEM_LIMIT),
        name="ev_out_ffn",
    )(xc, xs, mix_c, mix_s, mods, ngf, wo, w1, w2)


def _odd_kernel(tl, ctx_rows, smp_rows, x_ref, mod_ref, ngm_ref, ngf_ref, gfin_ref, ws_ref, gb_ref,
                cw_ref, wi_ref, wo_ref, w1_ref, w2_ref, yc_ref, ys_ref):
    s = pl.program_id(0)
    is_ctx = s < tl.n_ctx
    x = x_ref[...]
    mod = tl.mod_vecs(s, mod_ref)
    h = _rms_mod(x, ngm_ref[0], mod[1], mod[0]).astype(BF)
    v = _gelu_tanh(_dot(h, wi_ref[:, GW:2 * GW])).astype(BF)
    u = _gelu_tanh(_dot(h, wi_ref[:, 0:GW]))
    z = (_dot(h, wi_ref[:, 2 * GW + CW:2 * GW + 2 * CW])
         * _dot(h, wi_ref[:, 2 * GW + 2 * CW:2 * GW + 3 * CW]))
    gate_b = _dot(h, wi_ref[:, 2 * GW:2 * GW + CW])

    sp_rows = []
    for c in range(TM // GCH):
        cols = []
        for gi in range(GW // GCH):
            vg = v[c * GCH:(c + 1) * GCH, gi * GCH:(gi + 1) * GCH]
            cols.append(_dot(ws_ref[0, gi].astype(BF), vg))
        sp_rows.append(jnp.concatenate(cols, axis=1) + gb_ref[...])
    out_c = u * jnp.concatenate(sp_rows, axis=0)

    last = jnp.where(is_ctx, ctx_rows - 1, smp_rows - 1)
    pos = lax.broadcasted_iota(jnp.int32, (TM, CW), 0) & last
    zl = jnp.where(pos >= 1, pltpu.roll(z, 1, axis=0), 0.0)
    zr = jnp.where(pos < last, pltpu.roll(z, TM - 1, axis=0), 0.0)
    cw = cw_ref[0]
    out_d = gate_b * (zl * cw[0:1] + z * cw[1:2] + zr * cw[2:3])

    mix = jnp.concatenate([out_c, out_d], axis=1).astype(BF)
    x1 = x + mod[2] * _dot(mix, wo_ref[...])
    x2 = _ffn(x1, mod, ngf_ref[0], w1_ref, w2_ref)
    ms = jnp.mean(x2 * x2, axis=-1, keepdims=True)
    y = x2 * lax.rsqrt(ms + EPS) * gfin_ref[...]

    @pl.when(is_ctx)
    def _store_ctx():
        yc_ref[...] = y

    @pl.when(jnp.logical_not(is_ctx))
    def _store_smp():
        ys_ref[...] = y


def _odd(tl, ctx_rows, smp_rows, x, mods, ngm, ngf, gfin, gmlp_ws, gbias, conv_w, wi, wo, w1, w2,
         layer):
    for r in (ctx_rows, smp_rows):
        assert TM % r == 0 and r & (r - 1) == 0
    assert TM % GCH == 0
    return pl.pallas_call(
        functools.partial(_odd_kernel, tl, ctx_rows, smp_rows),
        grid=(tl.steps,),
        in_specs=tl.specs(D, False) + [
            tl.mod_spec(), tl.whole(ngm.shape, layer), tl.whole(ngf.shape, layer),
            tl.whole(gfin.shape), tl.whole(gmlp_ws.shape), tl.whole(gbias.shape),
            tl.whole(conv_w.shape),
            tl.weight(wi.shape), tl.weight(wo.shape), tl.weight(w1.shape), tl.weight(w2.shape)],
        out_specs=tl.specs(D, True),
        out_shape=[jax.ShapeDtypeStruct((tl.n_ctx * TM, D), F32),
                   jax.ShapeDtypeStruct((tl.n_smp * TM, D), F32)],
        compiler_params=pltpu.CompilerParams(
            dimension_semantics=("arbitrary",), vmem_limit_bytes=VMEM_LIMIT),
        name="odd_ffn_final",
    )(x, mods, ngm, ngf, gfin, gmlp_ws, gbias, conv_w, wi, wo, w1, w2)


def _dft_pos(l):
    n = np.arange(l)
    ang = 2.0 * np.pi * ((n[:, None] * n[None, :]) % l) / l
    to_bf = lambda a: jnp.asarray(a, F32).astype(BF)
    return to_bf(np.cos(ang) / np.sqrt(l)), to_bf(-np.sin(ang) / np.sqrt(l))


def _dft_chan():
    m = np.arange(FG)
    angc = 2.0 * np.pi * ((m[:, None] * m[None, :]) % FG) / FG
    eye = np.eye(FW // FG)
    cc = np.kron(eye, np.cos(angc)) / np.sqrt(FG)
    sc = np.kron(eye, np.sin(angc)) / np.sqrt(FG)
    return jnp.asarray(np.concatenate([cc, sc], axis=1), F32).astype(BF)


def kernel(x_prompt, x_sample, state_gla_fwd, state_gla_bwd, c, c_ctx, ada_w, ada_b, norm_mix_g,
           norm_ffn_g, ffn_w1, ffn_w2, ev_w_in, ev_w_out, gla_w2_f, gla_b2_f, gla_w2_b, gla_b2_b,
           gla_norm_g, od_w_in, od_w_out, gmlp_ws, gmlp_b, conv_w, final_norm_g):
    b_ctx, l_ctx, _ = x_prompt.shape
    b_smp, l_smp, _ = x_sample.shape
    depth = ada_w.shape[0]
    assert TM % l_ctx == 0 and l_smp % TM == 0 and b_smp % SUB == 0
    n_ctx, n_smp, tps = b_ctx * l_ctx // TM, b_smp * l_smp // TM, l_smp // TM
    tl = _Tiles(n_ctx, n_smp, tps)

    mod_args = (c_ctx.reshape(1, D), c, ada_w, ada_b.reshape(depth, 1, N_MOD * D))
    ngm = norm_mix_g.reshape(depth, 1, D)
    ngf = norm_ffn_g.reshape(depth, 1, D)
    gbias = jnp.repeat(gmlp_b[0].T, GCH, axis=1)
    xc = x_prompt.reshape(b_ctx * l_ctx, D)
    xs = x_sample.reshape(b_smp * l_smp, D)

    q, k, la, v, g, xcs, mods0 = _ev_in(
        _Tiles(n_ctx, n_smp, tps, lead=D // LANE), xc, xs, mod_args + (0,), ngm,
        jnp.swapaxes(ev_w_in, 1, 2), gla_w2_f, gla_w2_b, gla_b2_f, gla_b2_b, _dft_chan())
    mix_c, sf, sb, wo0, w1_0, w2_0 = _gla(
        b_ctx, l_ctx, CTX_SEQS, 0, q, k, la, v, g, xcs, gla_norm_g, *_dft_pos(l_ctx), None,
        [(ev_w_out, 0), (ffn_w1, 0), (ffn_w2, 0)])
    mix_s, wi1, wo1, w1_1, w2_1, mods1 = _gla(
        b_smp, l_smp, 1, b_ctx * l_ctx, q, k, la, v, g, xcs, gla_norm_g, *_dft_pos(l_smp),
        (jnp.swapaxes(state_gla_fwd, -1, -2), jnp.swapaxes(state_gla_bwd, -1, -2)),
        [(od_w_in, 0), (od_w_out, 0), (ffn_w1, 1), (ffn_w2, 1)], mod=mod_args + (1,))
    x1 = _ev_out(tl, xc, xs, mix_c, mix_s, mods0, ngf, wo0, w1_0, w2_0, 0)

    yc, ys = _odd(tl, l_ctx, GRID_W, x1, mods1, ngm, ngf, final_norm_g.reshape(1, D), gmlp_ws, gbias,
                  conv_w, wi1, wo1, w1_1, w2_1, 1)
    return (yc.reshape(b_ctx, l_ctx, D), ys.reshape(b_smp, l_smp, D),
            jnp.swapaxes(sf, -1, -2), jnp.swapaxes(sb, -1, -2))
```

```python
import functools

import jax
import jax.numpy as jnp
import numpy as np
from jax import lax
from jax.experimental import pallas as pl
from jax.experimental.pallas import tpu as pltpu

D = 1024
D_FF = 4 * D
EPS = 1e-6
N_MOD = 6
GRID_W = 64

H = 4
DK = 128
DV = 192
GLA_LOWRANK = 16
GLA_TAU = 16.0
CH = 64
FW = 256
FG = 64
GW = 512
GCH = 128
CW = 512

EV_V0 = 2 * H * DK
EV_G0 = EV_V0 + H * DV
EV_LR0 = EV_G0 + H * DV
EV_FIN0 = EV_LR0 + 2 * GLA_LOWRANK
EV_IN = EV_FIN0 + FW

TM = 512
FF_SPLIT = 2
LRP = 128
PG = 256
ROW_SPLIT = 2
MIX0 = H * DV + FW
OD_COLS = 2 * GW + 3 * CW
GLA_UNROLL = 16
CTX_SEQS = 2
SUB = 8
LANE = 128

VMEM_LIMIT = 62 * 1024 * 1024

BF = jnp.bfloat16
F32 = jnp.float32


def _dot(a, b):
    return jnp.dot(a, b, preferred_element_type=F32)


def _dot_nt(a, b):
    return lax.dot_general(a, b, (((1,), (1,)), ((), ())), preferred_element_type=F32)


def _dot_tn(a, b):
    return lax.dot_general(a, b, (((0,), (0,)), ((), ())), preferred_element_type=F32)


def _sigmoid(x):
    return 1.0 / (1.0 + jnp.exp(-x))


def _silu(x):
    return x * _sigmoid(x)


def _log_sigmoid(x):
    return jnp.minimum(x, 0.0) - jnp.log(1.0 + jnp.exp(-jnp.abs(x)))


def _gelu_tanh(x):
    c = np.float32(np.sqrt(2.0 / np.pi))
    return 0.5 * x * (1.0 + jnp.tanh(c * (x + 0.044715 * (x * x * x))))


def _rms_mod(x, g, sc, sh):
    ms = jnp.mean(x * x, axis=-1, keepdims=True)
    return (x * lax.rsqrt(ms + EPS)) * (g * (1.0 + sc)) + sh


def _pair_heads(f, b):
    return jnp.concatenate([p for hh in range(H)
                            for p in (f[:, hh * DK:(hh + 1) * DK], b[:, hh * DK:(hh + 1) * DK])], axis=1)


class _Tiles:
    def __init__(self, n_ctx, n_smp, tiles_per_smp_seq, lead=0):
        self.n_ctx, self.n_smp, self.tps, self.lead = n_ctx, n_smp, tiles_per_smp_seq, lead
        self.steps = lead + n_ctx + n_smp

    def tile(self, s):
        return jnp.maximum(s - self.lead, 0)

    def ctx(self, s):
        return jnp.minimum(self.tile(s), self.n_ctx - 1)

    def smp(self, s):
        return jnp.maximum(self.tile(s) - self.n_ctx, 0)

    def cond_row(self, s):
        return jnp.where(self.tile(s) < self.n_ctx, 0, SUB + self.smp(s) // self.tps)

    def specs(self, width, dual):
        if dual:
            return [pl.BlockSpec((TM, width), lambda s: (self.ctx(s), 0)),
                    pl.BlockSpec((TM, width), lambda s: (self.smp(s), 0))]
        return [pl.BlockSpec((TM, width), lambda s: (self.tile(s), 0))]

    def mod_spec(self):
        return pl.BlockSpec((1, SUB, N_MOD * D), lambda s: (0, self.cond_row(s) // SUB, 0))

    def mod_vecs(self, s, mod_ref):
        row = mod_ref[0, pl.ds(self.cond_row(s) % SUB, 1), :]
        return [row[:, i * D:(i + 1) * D] for i in range(N_MOD)]

    def whole(self, shape, layer=None):
        if layer is None:
            return pl.BlockSpec(shape, lambda s: tuple(0 for _ in shape))
        return pl.BlockSpec((1,) + shape[1:], lambda s: (layer,) + tuple(0 for _ in shape[1:]))

    def weight(self, shape):
        return pl.BlockSpec(shape, lambda s: (0, 0), pipeline_mode=pl.Buffered(1))


def _mod_kernel(cx_ref, c_ref, w_ref, b_ref, o_ref):
    w = w_ref[0].astype(BF)
    ax = jnp.broadcast_to(_silu(cx_ref[...]), (SUB, D)).astype(BF)
    o_ref[0, 0:SUB, :] = _dot(ax, w) + b_ref[0]
    o_ref[0, SUB:, :] = _dot(_silu(c_ref[...]).astype(BF), w) + b_ref[0]


def _mod_specs(c, layer, steps, block=lambda j: j):
    tn = N_MOD * D // steps
    assert tn % LANE == 0
    return ([pl.BlockSpec((1, D), lambda j: (0, 0)),
             pl.BlockSpec(c.shape, lambda j: (0, 0)),
             pl.BlockSpec((1, D, tn), lambda j: (layer, 0, block(j))),
             pl.BlockSpec((1, 1, tn), lambda j: (layer, 0, block(j)))],
            pl.BlockSpec((1, SUB + c.shape[0], tn), lambda j: (0, 0, block(j))),
            jax.ShapeDtypeStruct((1, SUB + c.shape[0], N_MOD * D), F32))


def _ev_in_kernel(tl, xc_ref, xs_ref, cx_ref, c_ref, aw_ref, ab_ref, ng_ref, wc_ref, w2f_ref,
                  w2b_ref, b2f_ref, b2b_ref, dft_ref,
                  q_ref, k_ref, la_ref, v_ref, g_ref, xcs_ref, mods_ref,
                  wqk_ref, wvg_ref, wlr_ref, wfin_ref, w2_ref, mod_s):
    s = pl.program_id(0)
    c0 = H * DK

    @pl.when(s < N_MOD)
    def _modulation():
        _mod_kernel(cx_ref, c_ref, aw_ref, ab_ref, mods_ref)
        mod_s[s] = mods_ref[0]

    @pl.when(s == 0)
    def _gate_weights():
        z = jnp.zeros((GLA_LOWRANK, c0), F32)
        w2_ref[...] = jnp.zeros_like(w2_ref)
        w2_ref[0:GLA_LOWRANK, :] = _pair_heads(w2f_ref[0], z).astype(BF)
        w2_ref[GLA_LOWRANK:2 * GLA_LOWRANK, :] = _pair_heads(z, w2b_ref[0]).astype(BF)

    @pl.when(s < tl.lead)
    def _cast_weights():
        w = wc_ref[0]
        r0 = pl.multiple_of(s * LANE, LANE)
        rows = pl.ds(r0, LANE)

        def tr(lo, hi, pad):
            piece = w[lo:hi, :]
            if pad:
                piece = jnp.concatenate([piece, jnp.zeros((pad, LANE), F32)], axis=0)
            return piece.T.astype(BF)

        wqk_ref[rows, :] = tr(0, EV_V0, 0)
        wvg_ref[rows, :] = tr(EV_V0, EV_LR0, 0)
        wlr_ref[rows, :] = tr(EV_LR0, EV_FIN0, LRP - 2 * GLA_LOWRANK)
        wfin_ref[rows, :] = tr(EV_FIN0, EV_IN, 0)

    @pl.when(s >= tl.lead)
    def _tile():
        is_ctx = s - tl.lead < tl.n_ctx
        crow = tl.cond_row(s)
        b2 = _pair_heads(b2f_ref[...], b2b_ref[...])
        nv = H * DV // PG
        for r in range(ROW_SPLIT):
            rs = slice(r * TM // ROW_SPLIT, (r + 1) * TM // ROW_SPLIT)
            x = jnp.where(is_ctx, xc_ref[rs, :], xs_ref[rs, :])
            h = _rms_mod(x, ng_ref[0], mod_s[1, pl.ds(crow, 1), :],
                         mod_s[0, pl.ds(crow, 1), :]).astype(BF)

            lr = _dot(h, wlr_ref[...]).astype(BF)
            for j in range(2 * nv):
                cs = slice(j * PG, (j + 1) * PG)
                if j < 2 * c0 // PG:
                    pre = _dot(lr, w2_ref[:, cs]) + b2[:, cs]
                    la_ref[rs, cs] = _log_sigmoid(pre) * np.float32(1.0 / GLA_TAU)
                out_ref, oc = (v_ref, j) if j < nv else (g_ref, j - nv)
                out_ref[rs, oc * PG:(oc + 1) * PG] = _dot(h, wvg_ref[:, cs]).astype(BF)
            q_ref[rs, :] = (_dot(h, wqk_ref[:, 0:c0]) * np.float32(DK ** -0.5)).astype(BF)
            k_ref[rs, :] = _dot(h, wqk_ref[:, c0:2 * c0]).astype(BF)
            fin = _dot(h, wfin_ref[...]).astype(BF)
            xcs_ref[rs, :] = _dot(fin, dft_ref[...]).astype(BF)


def _ev_in(tl, xc, xs, mod, ngm, ev_w_in_t, w2f, w2b, b2f, b2b, dftc):
    c0 = H * DK
    assert tl.lead == D // LANE >= N_MOD and (H * DV) % PG == 0 and 2 * DK == PG
    ntok = (tl.n_ctx + tl.n_smp) * TM
    widths = [(c0, BF), (c0, BF), (2 * c0, F32), (H * DV, BF), (H * DV, BF), (2 * FW, BF)]
    mod_in, mod_out, mod_shape = _mod_specs(mod[1], mod[4], N_MOD, lambda s: jnp.minimum(s, N_MOD - 1))
    return pl.pallas_call(
        functools.partial(_ev_in_kernel, tl),
        grid=(tl.steps,),
        in_specs=tl.specs(D, True) + mod_in + [
            tl.whole(ngm.shape, 0),
            pl.BlockSpec((1, EV_IN, LANE), lambda s: (0, 0, jnp.minimum(s, tl.lead - 1))),
            tl.whole(w2f.shape), tl.whole(w2b.shape), tl.whole(b2f.shape), tl.whole(b2b.shape),
            tl.whole(dftc.shape)],
        out_specs=[tl.specs(w, False)[0] for w, _ in widths] + [mod_out],
        out_shape=[jax.ShapeDtypeStruct((ntok, w), dt) for w, dt in widths] + [mod_shape],
        scratch_shapes=[pltpu.VMEM((D, 2 * c0), BF), pltpu.VMEM((D, 2 * H * DV), BF),
                        pltpu.VMEM((D, LRP), BF), pltpu.VMEM((D, FW), BF),
                        pltpu.VMEM((LRP, 2 * c0), BF),
                        pltpu.VMEM((N_MOD, SUB + mod[1].shape[0], D), F32)],
        compiler_params=pltpu.CompilerParams(
            dimension_semantics=("arbitrary",), vmem_limit_bytes=VMEM_LIMIT),
        name="ev_in",
    )(xc, xs, *mod[:4], ngm, ev_w_in_t, w2f, w2b, b2f, b2b, dftc)


def _chunk_scan(x, pos, reverse):
    s = 1
    while s < CH:
        if reverse:
            y = pltpu.roll(x, CH - s, axis=0)
            x = x + jnp.where(pos < CH - s, y, 0.0)
        else:
            y = pltpu.roll(x, s, axis=0)
            x = x + jnp.where(pos >= s, y, 0.0)
        s *= 2
    return x


def _gla_kernel(l, nseq, has_state, n_cast, with_mod, *refs):
    q_ref, k_ref, la_ref, v_ref, g_ref, xcs_ref, gn_ref, cl_ref, sl_ref = refs[:9]
    refs = refs[9:]
    if has_state:
        s0f_ref, s0b_ref = refs[:2]
        refs = refs[2:]
    cast_src, refs = refs[:n_cast], refs[n_cast:]
    if with_mod:
        mod_in, refs = refs[:4], refs[4:]
    mix_ref, refs = refs[0], refs[1:]
    if not has_state:
        sf_ref, sb_ref = refs[:2]
        refs = refs[2:]
    cast_dst, refs = refs[:n_cast], refs[n_cast:]
    if with_mod:
        _mod_kernel(*mod_in, refs[0])
        refs = refs[1:]
    bc_ref, ke_ref, dec_ref, u_ref, sp_ref, st_ref = refs

    for src, dst in zip(cast_src, cast_dst):
        dst[...] = src[0].astype(BF)

    nchunks = l // CH
    nall = nseq * nchunks
    unroll = min(GLA_UNROLL, nall)
    pos = lax.broadcasted_iota(jnp.int32, (CH, DK), 0)
    row = lax.broadcasted_iota(jnp.int32, (CH, CH), 0)
    col = lax.broadcasted_iota(jnp.int32, (CH, CH), 1)
    gn = gn_ref[...]

    for hh in range(H):
        ks = slice(hh * DK, (hh + 1) * DK)
        vs = slice(hh * DV, (hh + 1) * DV)

        def inc_body(n, carry):
            r0 = pl.multiple_of(n * CH, CH)
            la = la_ref[pl.ds(r0, CH), hh * 2 * DK:(hh + 1) * 2 * DK]
            bf = _chunk_scan(la[:, 0:DK], pos, False)
            bb = _chunk_scan(la[:, DK:2 * DK], pos, True)
            bc_ref[pl.ds(r0, CH), 0:DK] = bf
            bc_ref[pl.ds(r0, CH), DK:2 * DK] = bb
            dec_ref[n, :, 0:DK] = jnp.exp(bf[CH - 1:CH, :])
            dec_ref[n, :, DK:2 * DK] = jnp.exp(bb[0:1, :])
            kk = k_ref[pl.ds(r0, CH), ks].astype(F32)
            ke = jnp.concatenate([kk * jnp.exp(-bf), kk * jnp.exp(-bb)], axis=1).astype(BF)
            ke_ref[pl.ds(r0, CH), :] = ke
            u_ref[n] = _dot_tn(v_ref[pl.ds(r0, CH), vs], ke)
            return carry

        lax.fori_loop(0, nall, inc_body, 0, unroll=unroll)

        for sq in range(nseq):
            if has_state:
                st_ref[:, 0:DK] = s0f_ref[sq, 0, hh]
                st_ref[:, DK:2 * DK] = s0b_ref[sq, 0, hh]
            else:
                st_ref[...] = jnp.zeros_like(st_ref)

            def scan_body(i, carry, sq=sq):
                nf = sq * nchunks + i
                nb = sq * nchunks + nchunks - 1 - i
                sf = st_ref[:, 0:DK]
                sb = st_ref[:, DK:2 * DK]
                sp_ref[nf, :, 0:DK] = sf.astype(BF)
                sp_ref[nb, :, DK:2 * DK] = sb.astype(BF)
                st_ref[:, 0:DK] = dec_ref[nf, :, 0:DK] * (sf + u_ref[nf, :, 0:DK])
                st_ref[:, DK:2 * DK] = dec_ref[nb, :, DK:2 * DK] * (sb + u_ref[nb, :, DK:2 * DK])
                return carry

            lax.fori_loop(0, nchunks, scan_body, 0)
            if not has_state:
                sf_ref[sq, 0, hh] = st_ref[:, 0:DK]
                sb_ref[sq, 0, hh] = st_ref[:, DK:2 * DK]

        def out_body(n, carry):
            r0 = pl.multiple_of(n * CH, CH)
            bc = bc_ref[pl.ds(r0, CH), :]
            qq = q_ref[pl.ds(r0, CH), ks].astype(F32)
            qe = jnp.concatenate([qq * jnp.exp(bc[:, 0:DK]), qq * jnp.exp(bc[:, DK:2 * DK])],
                                 axis=1).astype(BF)
            ke = ke_ref[pl.ds(r0, CH), :]
            att = (jnp.where(row >= col, _dot_nt(qe[:, 0:DK], ke[:, 0:DK]), 0.0)
                   + jnp.where(row <= col, _dot_nt(qe[:, DK:], ke[:, DK:]), 0.0)).astype(BF)
            o = _dot(att, v_ref[pl.ds(r0, CH), vs]) + _dot_nt(qe, sp_ref[n])
            ms = jnp.mean(o * o, axis=-1, keepdims=True)
            on = o * lax.rsqrt(ms + EPS) * gn
            gate = _silu(g_ref[pl.ds(r0, CH), vs].astype(F32))
            mix_ref[pl.ds(r0, CH), vs] = (on * gate).astype(BF)
            return carry

        lax.fori_loop(0, nall, out_body, 0, unroll=unroll)

    for sq in range(nseq):
        rs = slice(sq * l, (sq + 1) * l)
        fo = _dot(cl_ref[...], xcs_ref[rs, 0:FW]) + _dot(sl_ref[...], xcs_ref[rs, FW:2 * FW])
        mix_ref[rs, H * DV:H * DV + FW] = fo.astype(BF)


def _gla(b, l, nseq, tok0, q, k, la, v, g, xcs, gn, cl, sl, states, casts, mod=None):
    nchunks = l // CH
    c0 = H * DK
    rows = nseq * l
    steps = b // nseq
    assert tok0 % rows == 0 and b % nseq == 0
    seq = lambda w: pl.BlockSpec((rows, w), lambda i: (tok0 // rows + i, 0))
    const = lambda shape: pl.BlockSpec(shape, lambda i: tuple(0 for _ in shape))
    once = lambda shape: pl.BlockSpec(shape, lambda i: (0, 0), pipeline_mode=pl.Buffered(1))
    st_spec = pl.BlockSpec((nseq, 1, H, DV, DK), lambda i: (i, 0, 0, 0, 0))
    in_specs = [seq(c0), seq(c0), seq(2 * c0), seq(H * DV), seq(H * DV), seq(2 * FW),
                const(gn.shape), once((l, l)), once((l, l))]
    out_specs = [pl.BlockSpec((rows, MIX0), lambda i: (i, 0))]
    out_shape = [jax.ShapeDtypeStruct((b * l, MIX0), BF)]
    args = [q, k, la, v, g, xcs, gn, cl, sl]
    if states is not None:
        in_specs += [st_spec, st_spec]
        args += list(states)
    else:
        out_specs += [st_spec, st_spec]
        out_shape += [jax.ShapeDtypeStruct((b, 1, H, DV, DK), F32)] * 2
    for w, layer in casts:
        _, r, c = w.shape
        assert r % steps == 0
        in_specs.append(pl.BlockSpec((1, r // steps, c), lambda i, layer=layer: (layer, i, 0)))
        args.append(w)
        out_specs.append(pl.BlockSpec((r // steps, c), lambda i: (i, 0)))
        out_shape.append(jax.ShapeDtypeStruct((r, c), BF))
    if mod is not None:
        mod_in, mod_out, mod_shape = _mod_specs(mod[1], mod[4], steps)
        in_specs += mod_in
        args += list(mod[:4])
        out_specs.append(mod_out)
        out_shape.append(mod_shape)
    return pl.pallas_call(
        functools.partial(_gla_kernel, l, nseq, states is not None, len(casts), mod is not None),
        grid=(steps,),
        in_specs=in_specs,
        out_specs=out_specs,
        out_shape=out_shape,
        scratch_shapes=[pltpu.VMEM((rows, 2 * DK), F32),
                        pltpu.VMEM((rows, 2 * DK), BF),
                        pltpu.VMEM((nseq * nchunks, 1, 2 * DK), F32),
                        pltpu.VMEM((nseq * nchunks, DV, 2 * DK), F32),
                        pltpu.VMEM((nseq * nchunks, DV, 2 * DK), BF),
                        pltpu.VMEM((DV, 2 * DK), F32)],
        compiler_params=pltpu.CompilerParams(
            dimension_semantics=("arbitrary",), vmem_limit_bytes=VMEM_LIMIT),
        name="gla_fnet",
    )(*args)


def _ffn(x1, mod, ng, w1_ref, w2_ref):
    h = _rms_mod(x1, ng, mod[4], mod[3]).astype(BF)
    wd = D_FF // FF_SPLIT
    out = None
    for c in range(FF_SPLIT):
        hid = jnp.maximum(_dot(h, w1_ref[:, c * wd:(c + 1) * wd]), 0.0)
        part = _dot((hid * hid).astype(BF), w2_ref[c * wd:(c + 1) * wd, :])
        out = part if out is None else out + part
    return x1 + mod[5] * out


def _ev_out_kernel(tl, xc_ref, xs_ref, mc_ref, ms_ref, mod_ref, ng_ref, wo_ref, w1_ref, w2_ref,
                   o_ref):
    s = pl.program_id(0)
    is_ctx = s < tl.n_ctx
    mod = tl.mod_vecs(s, mod_ref)
    x = jnp.where(is_ctx, xc_ref[...], xs_ref[...])
    mix = jnp.where(is_ctx, mc_ref[...], ms_ref[...])
    x1 = x + mod[2] * _dot(mix, wo_ref[...])
    o_ref[...] = _ffn(x1, mod, ng_ref[0], w1_ref, w2_ref)


def _ev_out(tl, xc, xs, mix_c, mix_s, mods, ngf, wo, w1, w2, layer):
    ntok = (tl.n_ctx + tl.n_smp) * TM
    return pl.pallas_call(
        functools.partial(_ev_out_kernel, tl),
        grid=(tl.steps,),
        in_specs=tl.specs(D, True) + tl.specs(MIX0, True) + [
            tl.mod_spec(), tl.whole(ngf.shape, layer),
            tl.weight(wo.shape), tl.weight(w1.shape), tl.weight(w2.shape)],
        out_specs=tl.specs(D, False)[0],
        out_shape=jax.ShapeDtypeStruct((ntok, D), F32),
        compiler_params=pltpu.CompilerParams(
            dimension_semantics=("arbitrary",), vmem_limit_bytes=VMEM_LIMIT),
        name="ev_out_ffn",
    )(xc, xs, mix_c, mix_s, mods, ngf, wo, w1, w2)


def _odd_kernel(tl, ctx_rows, smp_rows, x_ref, mod_ref, ngm_ref, ngf_ref, gfin_ref, ws_ref, gb_ref,
                cw_ref, wi_ref, wo_ref, w1_ref, w2_ref, yc_ref, ys_ref):
    s = pl.program_id(0)
    is_ctx = s < tl.n_ctx
    x = x_ref[...]
    mod = tl.mod_vecs(s, mod_ref)
    h = _rms_mod(x, ngm_ref[0], mod[1], mod[0]).astype(BF)
    v = _gelu_tanh(_dot(h, wi_ref[:, GW:2 * GW])).astype(BF)
    u = _gelu_tanh(_dot(h, wi_ref[:, 0:GW]))
    z = (_dot(h, wi_ref[:, 2 * GW + CW:2 * GW + 2 * CW])
         * _dot(h, wi_ref[:, 2 * GW + 2 * CW:2 * GW + 3 * CW]))
    gate_b = _dot(h, wi_ref[:, 2 * GW:2 * GW + CW])

    sp_rows = []
    for c in range(TM // GCH):
        cols = []
        for gi in range(GW // GCH):
            vg = v[c * GCH:(c + 1) * GCH, gi * GCH:(gi + 1) * GCH]
            cols.append(_dot(ws_ref[0, gi].astype(BF), vg))
        sp_rows.append(jnp.concatenate(cols, axis=1) + gb_ref[...])
    out_c = u * jnp.concatenate(sp_rows, axis=0)

    last = jnp.where(is_ctx, ctx_rows - 1, smp_rows - 1)
    pos = lax.broadcasted_iota(jnp.int32, (TM, CW), 0) & last
    zl = jnp.where(pos >= 1, pltpu.roll(z, 1, axis=0), 0.0)
    zr = jnp.where(pos < last, pltpu.roll(z, TM - 1, axis=0), 0.0)
    cw = cw_ref[0]
    out_d = gate_b * (zl * cw[0:1] + z * cw[1:2] + zr * cw[2:3])

    mix = jnp.concatenate([out_c, out_d], axis=1).astype(BF)
    x1 = x + mod[2] * _dot(mix, wo_ref[...])
    x2 = _ffn(x1, mod, ngf_ref[0], w1_ref, w2_ref)
    ms = jnp.mean(x2 * x2, axis=-1, keepdims=True)
    y = x2 * lax.rsqrt(ms + EPS) * gfin_ref[...]

    @pl.when(is_ctx)
    def _store_ctx():
        yc_ref[...] = y

    @pl.when(jnp.logical_not(is_ctx))
    def _store_smp():
        ys_ref[...] = y


def _odd(tl, ctx_rows, smp_rows, x, mods, ngm, ngf, gfin, gmlp_ws, gbias, conv_w, wi, wo, w1, w2,
         layer):
    for r in (ctx_rows, smp_rows):
        assert TM % r == 0 and r & (r - 1) == 0
    assert TM % GCH == 0
    return pl.pallas_call(
        functools.partial(_odd_kernel, tl, ctx_rows, smp_rows),
        grid=(tl.steps,),
        in_specs=tl.specs(D, False) + [
            tl.mod_spec(), tl.whole(ngm.shape, layer), tl.whole(ngf.shape, layer),
            tl.whole(gfin.shape), tl.whole(gmlp_ws.shape), tl.whole(gbias.shape),
            tl.whole(conv_w.shape),
            tl.weight(wi.shape), tl.weight(wo.shape), tl.weight(w1.shape), tl.weight(w2.shape)],
        out_specs=tl.specs(D, True),
        out_shape=[jax.ShapeDtypeStruct((tl.n_ctx * TM, D), F32),
                   jax.ShapeDtypeStruct((tl.n_smp * TM, D), F32)],
        compiler_params=pltpu.CompilerParams(
            dimension_semantics=("arbitrary",), vmem_limit_bytes=VMEM_LIMIT),
        name="odd_ffn_final",
    )(x, mods, ngm, ngf, gfin, gmlp_ws, gbias, conv_w, wi, wo, w1, w2)


def _dft_pos(l):
    n = np.arange(l)
    ang = 2.0 * np.pi * ((n[:, None] * n[None, :]) % l) / l
    to_bf = lambda a: jnp.asarray(a, F32).astype(BF)
    return to_bf(np.cos(ang) / np.sqrt(l)), to_bf(-np.sin(ang) / np.sqrt(l))


def _dft_chan():
    m = np.arange(FG)
    angc = 2.0 * np.pi * ((m[:, None] * m[None, :]) % FG) / FG
    eye = np.eye(FW // FG)
    cc = np.kron(eye, np.cos(angc)) / np.sqrt(FG)
    sc = np.kron(eye, np.sin(angc)) / np.sqrt(FG)
    return jnp.asarray(np.concatenate([cc, sc], axis=1), F32).astype(BF)


def kernel(x_prompt, x_sample, state_gla_fwd, state_gla_bwd, c, c_ctx, ada_w, ada_b, norm_mix_g,
           norm_ffn_g, ffn_w1, ffn_w2, ev_w_in, ev_w_out, gla_w2_f, gla_b2_f, gla_w2_b, gla_b2_b,
           gla_norm_g, od_w_in, od_w_out, gmlp_ws, gmlp_b, conv_w, final_norm_g):
    b_ctx, l_ctx, _ = x_prompt.shape
    b_smp, l_smp, _ = x_sample.shape
    depth = ada_w.shape[0]
    assert TM % l_ctx == 0 and l_smp % TM == 0 and b_smp % SUB == 0
    n_ctx, n_smp, tps = b_ctx * l_ctx // TM, b_smp * l_smp // TM, l_smp // TM
    tl = _Tiles(n_ctx, n_smp, tps)

    mod_args = (c_ctx.reshape(1, D), c, ada_w, ada_b.reshape(depth, 1, N_MOD * D))
    ngm = norm_mix_g.reshape(depth, 1, D)
    ngf = norm_ffn_g.reshape(depth, 1, D)
    gbias = jnp.repeat(gmlp_b[0].T, GCH, axis=1)
    xc = x_prompt.reshape(b_ctx * l_ctx, D)
    xs = x_sample.reshape(b_smp * l_smp, D)

    q, k, la, v, g, xcs, mods0 = _ev_in(
        _Tiles(n_ctx, n_smp, tps, lead=D // LANE), xc, xs, mod_args + (0,), ngm,
        jnp.swapaxes(ev_w_in, 1, 2), gla_w2_f, gla_w2_b, gla_b2_f, gla_b2_b, _dft_chan())
    mix_c, sf, sb, wo0, w1_0, w2_0 = _gla(
        b_ctx, l_ctx, CTX_SEQS, 0, q, k, la, v, g, xcs, gla_norm_g, *_dft_pos(l_ctx), None,
        [(ev_w_out, 0), (ffn_w1, 0), (ffn_w2, 0)])
    mix_s, wi1, wo1, w1_1, w2_1, mods1 = _gla(
        b_smp, l_smp, 1, b_ctx * l_ctx, q, k, la, v, g, xcs, gla_norm_g, *_dft_pos(l_smp),
        (jnp.swapaxes(state_gla_fwd, -1, -2), jnp.swapaxes(state_gla_bwd, -1, -2)),
        [(od_w_in, 0), (od_w_out, 0), (ffn_w1, 1), (ffn_w2, 1)], mod=mod_args + (1,))
    x1 = _ev_out(tl, xc, xs, mix_c, mix_s, mods0, ngf, wo0, w1_0, w2_0, 0)

    yc, ys = _odd(tl, l_ctx, GRID_W, x1, mods1, ngm, ngf, final_norm_g.reshape(1, D), gmlp_ws, gbias,
                  conv_w, wi1, wo1, w1_1, w2_1, 1)
    return (yc.reshape(b_ctx, l_ctx, D), ys.reshape(b_smp, l_smp, D),
            jnp.swapaxes(sf, -1, -2), jnp.swapaxes(sb, -1, -2))
```

```python
import functools

import jax
import jax.numpy as jnp
import numpy as np
from jax import lax
from jax.experimental import pallas as pl
from jax.experimental.pallas import tpu as pltpu

D = 1024
D_FF = 4 * D
EPS = 1e-6
N_MOD = 6
GRID_W = 64

H = 4
DK = 128
DV = 192
GLA_LOWRANK = 16
GLA_TAU = 16.0
CH = 64
FW = 256
FG = 64
GW = 512
GCH = 128
CW = 512

EV_V0 = 2 * H * DK
EV_G0 = EV_V0 + H * DV
EV_LR0 = EV_G0 + H * DV
EV_FIN0 = EV_LR0 + 2 * GLA_LOWRANK
EV_IN = EV_FIN0 + FW

TM = 512
FF_SPLIT = 2
LRP = 128
PG = 256
ROW_SPLIT = 2
MIX0 = H * DV + FW
OD_COLS = 2 * GW + 3 * CW
GLA_UNROLL = 16
CTX_SEQS = 2
SUB = 8
LANE = 128

VMEM_LIMIT = 62 * 1024 * 1024

BF = jnp.bfloat16
F32 = jnp.float32


def _dot(a, b):
    return jnp.dot(a, b, preferred_element_type=F32)


def _dot_nt(a, b):
    return lax.dot_general(a, b, (((1,), (1,)), ((), ())), preferred_element_type=F32)


def _dot_tn(a, b):
    return lax.dot_general(a, b, (((0,), (0,)), ((), ())), preferred_element_type=F32)


def _sigmoid(x):
    return 1.0 / (1.0 + jnp.exp(-x))


def _silu(x):
    return x * _sigmoid(x)


def _log_sigmoid(x):
    return jnp.minimum(x, 0.0) - jnp.log(1.0 + jnp.exp(-jnp.abs(x)))


def _gelu_tanh(x):
    c = np.float32(np.sqrt(2.0 / np.pi))
    return 0.5 * x * (1.0 + jnp.tanh(c * (x + 0.044715 * (x * x * x))))


def _rms_mod(x, g, sc, sh):
    ms = jnp.mean(x * x, axis=-1, keepdims=True)
    return (x * lax.rsqrt(ms + EPS)) * (g * (1.0 + sc)) + sh


def _pair_heads(f, b):
    return jnp.concatenate([p for hh in range(H)
                            for p in (f[:, hh * DK:(hh + 1) * DK], b[:, hh * DK:(hh + 1) * DK])], axis=1)


class _Tiles:
    def __init__(self, n_ctx, n_smp, tiles_per_smp_seq, lead=0):
        self.n_ctx, self.n_smp, self.tps, self.lead = n_ctx, n_smp, tiles_per_smp_seq, lead
        self.steps = lead + n_ctx + n_smp

    def tile(self, s):
        return jnp.maximum(s - self.lead, 0)

    def ctx(self, s):
        return jnp.minimum(self.tile(s), self.n_ctx - 1)

    def smp(self, s):
        return jnp.maximum(self.tile(s) - self.n_ctx, 0)

    def cond_row(self, s):
        return jnp.where(self.tile(s) < self.n_ctx, 0, SUB + self.smp(s) // self.tps)

    def specs(self, width, dual):
        if dual:
            return [pl.BlockSpec((TM, width), lambda s: (self.ctx(s), 0)),
                    pl.BlockSpec((TM, width), lambda s: (self.smp(s), 0))]
        return [pl.BlockSpec((TM, width), lambda s: (self.tile(s), 0))]

    def mod_spec(self):
        return pl.BlockSpec((1, SUB, N_MOD * D), lambda s: (0, self.cond_row(s) // SUB, 0))

    def mod_vecs(self, s, mod_ref):
        row = mod_ref[0, pl.ds(self.cond_row(s) % SUB, 1), :]
        return [row[:, i * D:(i + 1) * D] for i in range(N_MOD)]

    def whole(self, shape, layer=None):
        if layer is None:
            return pl.BlockSpec(shape, lambda s: tuple(0 for _ in shape))
        return pl.BlockSpec((1,) + shape[1:], lambda s: (layer,) + tuple(0 for _ in shape[1:]))

    def weight(self, shape):
        return pl.BlockSpec(shape, lambda s: (0, 0), pipeline_mode=pl.Buffered(1))


def _mod_kernel(cx_ref, c_ref, w_ref, b_ref, o_ref):
    w = w_ref[0].astype(BF)
    ax = jnp.broadcast_to(_silu(cx_ref[...]), (SUB, D)).astype(BF)
    o_ref[0, 0:SUB, :] = _dot(ax, w) + b_ref[0]
    o_ref[0, SUB:, :] = _dot(_silu(c_ref[...]).astype(BF), w) + b_ref[0]


def _mod_specs(c, layer, steps, block=lambda j: j):
    tn = N_MOD * D // steps
    assert tn % LANE == 0
    return ([pl.BlockSpec((1, D), lambda j: (0, 0)),
             pl.BlockSpec(c.shape, lambda j: (0, 0)),
             pl.BlockSpec((1, D, tn), lambda j: (layer, 0, block(j))),
             pl.BlockSpec((1, 1, tn), lambda j: (layer, 0, block(j)))],
            pl.BlockSpec((1, SUB + c.shape[0], tn), lambda j: (0, 0, block(j))),
            jax.ShapeDtypeStruct((1, SUB + c.shape[0], N_MOD * D), F32))


def _ev_in_kernel(tl, xc_ref, xs_ref, cx_ref, c_ref, aw_ref, ab_ref, ng_ref, wc_ref, w2f_ref,
                  w2b_ref, b2f_ref, b2b_ref, dft_ref,
                  q_ref, k_ref, la_ref, v_ref, g_ref, xcs_ref, mods_ref,
                  wqk_ref, wvg_ref, wlr_ref, wfin_ref, w2_ref, mod_s):
    s = pl.program_id(0)
    c0 = H * DK

    @pl.when(s < N_MOD)
    def _modulation():
        _mod_kernel(cx_ref, c_ref, aw_ref, ab_ref, mods_ref)
        mod_s[s] = mods_ref[0]

    @pl.when(s == 0)
    def _gate_weights():
        z = jnp.zeros((GLA_LOWRANK, c0), F32)
        w2_ref[...] = jnp.zeros_like(w2_ref)
        w2_ref[0:GLA_LOWRANK, :] = _pair_heads(w2f_ref[0], z).astype(BF)
        w2_ref[GLA_LOWRANK:2 * GLA_LOWRANK, :] = _pair_heads(z, w2b_ref[0]).astype(BF)

    @pl.when(s < tl.lead)
    def _cast_weights():
        w = wc_ref[0]
        r0 = pl.multiple_of(s * LANE, LANE)
        rows = pl.ds(r0, LANE)

        def tr(lo, hi, pad):
            piece = w[lo:hi, :]
            if pad:
                piece = jnp.concatenate([piece, jnp.zeros((pad, LANE), F32)], axis=0)
            return piece.T.astype(BF)

        wqk_ref[rows, :] = tr(0, EV_V0, 0)
        wvg_ref[rows, :] = tr(EV_V0, EV_LR0, 0)
        wlr_ref[rows, :] = tr(EV_LR0, EV_FIN0, LRP - 2 * GLA_LOWRANK)
        wfin_ref[rows, :] = tr(EV_FIN0, EV_IN, 0)

    @pl.when(s >= tl.lead)
    def _tile():
        is_ctx = s - tl.lead < tl.n_ctx
        crow = tl.cond_row(s)
        b2 = _pair_heads(b2f_ref[...], b2b_ref[...])
        nv = H * DV // PG
        for r in range(ROW_SPLIT):
            rs = slice(r * TM // ROW_SPLIT, (r + 1) * TM // ROW_SPLIT)
            x = jnp.where(is_ctx, xc_ref[rs, :], xs_ref[rs, :])
            h = _rms_mod(x, ng_ref[0], mod_s[1, pl.ds(crow, 1), :],
                         mod_s[0, pl.ds(crow, 1), :]).astype(BF)

            lr = _dot(h, wlr_ref[...]).astype(BF)
            for j in range(2 * nv):
                cs = slice(j * PG, (j + 1) * PG)
                if j < 2 * c0 // PG:
                    pre = _dot(lr, w2_ref[:, cs]) + b2[:, cs]
                    la_ref[rs, cs] = _log_sigmoid(pre) * np.float32(1.0 / GLA_TAU)
                out_ref, oc = (v_ref, j) if j < nv else (g_ref, j - nv)
                out_ref[rs, oc * PG:(oc + 1) * PG] = _dot(h, wvg_ref[:, cs]).astype(BF)
            q_ref[rs, :] = (_dot(h, wqk_ref[:, 0:c0]) * np.float32(DK ** -0.5)).astype(BF)
            k_ref[rs, :] = _dot(h, wqk_ref[:, c0:2 * c0]).astype(BF)
            fin = _dot(h, wfin_ref[...]).astype(BF)
            xcs_ref[rs, :] = _dot(fin, dft_ref[...]).astype(BF)


def _ev_in(tl, xc, xs, mod, ngm, ev_w_in_t, w2f, w2b, b2f, b2b, dftc):
    c0 = H * DK
    assert tl.lead == D // LANE >= N_MOD and (H * DV) % PG == 0 and 2 * DK == PG
    ntok = (tl.n_ctx + tl.n_smp) * TM
    widths = [(c0, BF), (c0, BF), (2 * c0, F32), (H * DV, BF), (H * DV, BF), (2 * FW, BF)]
    mod_in, mod_out, mod_shape = _mod_specs(mod[1], mod[4], N_MOD, lambda s: jnp.minimum(s, N_MOD - 1))
    return pl.pallas_call(
        functools.partial(_ev_in_kernel, tl),
        grid=(tl.steps,),
        in_specs=tl.specs(D, True) + mod_in + [
            tl.whole(ngm.shape, 0),
            pl.BlockSpec((1, EV_IN, LANE), lambda s: (0, 0, jnp.minimum(s, tl.lead - 1))),
            tl.whole(w2f.shape), tl.whole(w2b.shape), tl.whole(b2f.shape), tl.whole(b2b.shape),
            tl.whole(dftc.shape)],
        out_specs=[tl.specs(w, False)[0] for w, _ in widths] + [mod_out],
        out_shape=[jax.ShapeDtypeStruct((ntok, w), dt) for w, dt in widths] + [mod_shape],
        scratch_shapes=[pltpu.VMEM((D, 2 * c0), BF), pltpu.VMEM((D, 2 * H * DV), BF),
                        pltpu.VMEM((D, LRP), BF), pltpu.VMEM((D, FW), BF),
                        pltpu.VMEM((LRP, 2 * c0), BF),
                        pltpu.VMEM((N_MOD, SUB + mod[1].shape[0], D), F32)],
        compiler_params=pltpu.CompilerParams(
            dimension_semantics=("arbitrary",), vmem_limit_bytes=VMEM_LIMIT),
        name="ev_in",
    )(xc, xs, *mod[:4], ngm, ev_w_in_t, w2f, w2b, b2f, b2b, dftc)


def _chunk_scan(x, pos, reverse):
    s = 1
    while s < CH:
        if reverse:
            y = pltpu.roll(x, CH - s, axis=0)
            x = x + jnp.where(pos < CH - s, y, 0.0)
        else:
            y = pltpu.roll(x, s, axis=0)
            x = x + jnp.where(pos >= s, y, 0.0)
        s *= 2
    return x


def _gla_kernel(l, nseq, has_state, n_cast, with_mod, *refs):
    q_ref, k_ref, la_ref, v_ref, g_ref, xcs_ref, gn_ref, cl_ref, sl_ref = refs[:9]
    refs = refs[9:]
    if has_state:
        s0f_ref, s0b_ref = refs[:2]
        refs = refs[2:]
    cast_src, refs = refs[:n_cast], refs[n_cast:]
    if with_mod:
        mod_in, refs = refs[:4], refs[4:]
    mix_ref, refs = refs[0], refs[1:]
    if not has_state:
        sf_ref, sb_ref = refs[:2]
        refs = refs[2:]
    cast_dst, refs = refs[:n_cast], refs[n_cast:]
    if with_mod:
        _mod_kernel(*mod_in, refs[0])
        refs = refs[1:]
    bc_ref, ke_ref, dec_ref, u_ref, sp_ref, st_ref = refs

    for src, dst in zip(cast_src, cast_dst):
        dst[...] = src[0].astype(BF)

    nchunks = l // CH
    nall = nseq * nchunks
    unroll = min(GLA_UNROLL, nall)
    pos = lax.broadcasted_iota(jnp.int32, (CH, DK), 0)
    row = lax.broadcasted_iota(jnp.int32, (CH, CH), 0)
    col = lax.broadcasted_iota(jnp.int32, (CH, CH), 1)
    gn = gn_ref[...]

    for hh in range(H):
        ks = slice(hh * DK, (hh + 1) * DK)
        vs = slice(hh * DV, (hh + 1) * DV)

        def inc_body(n, carry):
            r0 = pl.multiple_of(n * CH, CH)
            la = la_ref[pl.ds(r0, CH), hh * 2 * DK:(hh + 1) * 2 * DK]
            bf = _chunk_scan(la[:, 0:DK], pos, False)
            bb = _chunk_scan(la[:, DK:2 * DK], pos, True)
            bc_ref[pl.ds(r0, CH), 0:DK] = bf
            bc_ref[pl.ds(r0, CH), DK:2 * DK] = bb
            dec_ref[n, :, 0:DK] = jnp.exp(bf[CH - 1:CH, :])
            dec_ref[n, :, DK:2 * DK] = jnp.exp(bb[0:1, :])
            kk = k_ref[pl.ds(r0, CH), ks].astype(F32)
            ke = jnp.concatenate([kk * jnp.exp(-bf), kk * jnp.exp(-bb)], axis=1).astype(BF)
            ke_ref[pl.ds(r0, CH), :] = ke
            u_ref[n] = _dot_tn(v_ref[pl.ds(r0, CH), vs], ke)
            return carry

        lax.fori_loop(0, nall, inc_body, 0, unroll=unroll)

        for sq in range(nseq):
            if has_state:
                st_ref[:, 0:DK] = s0f_ref[sq, 0, hh]
                st_ref[:, DK:2 * DK] = s0b_ref[sq, 0, hh]
            else:
                st_ref[...] = jnp.zeros_like(st_ref)

            def scan_body(i, carry, sq=sq):
                nf = sq * nchunks + i
                nb = sq * nchunks + nchunks - 1 - i
                sf = st_ref[:, 0:DK]
                sb = st_ref[:, DK:2 * DK]
                sp_ref[nf, :, 0:DK] = sf.astype(BF)
                sp_ref[nb, :, DK:2 * DK] = sb.astype(BF)
                st_ref[:, 0:DK] = dec_ref[nf, :, 0:DK] * (sf + u_ref[nf, :, 0:DK])
                st_ref[:, DK:2 * DK] = dec_ref[nb, :, DK:2 * DK] * (sb + u_ref[nb, :, DK:2 * DK])
                return carry

            lax.fori_loop(0, nchunks, scan_body, 0, unroll=4)
            if not has_state:
                sf_ref[sq, 0, hh] = st_ref[:, 0:DK]
                sb_ref[sq, 0, hh] = st_ref[:, DK:2 * DK]

        def out_body(n, carry):
            r0 = pl.multiple_of(n * CH, CH)
            bc = bc_ref[pl.ds(r0, CH), :]
            qq = q_ref[pl.ds(r0, CH), ks].astype(F32)
            qe = jnp.concatenate([qq * jnp.exp(bc[:, 0:DK]), qq * jnp.exp(bc[:, DK:2 * DK])],
                                 axis=1).astype(BF)
            ke = ke_ref[pl.ds(r0, CH), :]
            att = (jnp.where(row >= col, _dot_nt(qe[:, 0:DK], ke[:, 0:DK]), 0.0)
                   + jnp.where(row <= col, _dot_nt(qe[:, DK:], ke[:, DK:]), 0.0)).astype(BF)
            o = _dot(att, v_ref[pl.ds(r0, CH), vs]) + _dot_nt(qe, sp_ref[n])
            ms = jnp.mean(o * o, axis=-1, keepdims=True)
            on = o * lax.rsqrt(ms + EPS) * gn
            gate = _silu(g_ref[pl.ds(r0, CH), vs].astype(F32))
            mix_ref[pl.ds(r0, CH), vs] = (on * gate).astype(BF)
            return carry

        lax.fori_loop(0, nall, out_body, 0, unroll=unroll)

    for sq in range(nseq):
        rs = slice(sq * l, (sq + 1) * l)
        fo = _dot(cl_ref[...], xcs_ref[rs, 0:FW]) + _dot(sl_ref[...], xcs_ref[rs, FW:2 * FW])
        mix_ref[rs, H * DV:H * DV + FW] = fo.astype(BF)


def _gla(b, l, nseq, tok0, q, k, la, v, g, xcs, gn, cl, sl, states, casts, mod=None):
    nchunks = l // CH
    c0 = H * DK
    rows = nseq * l
    steps = b // nseq
    assert tok0 % rows == 0 and b % nseq == 0
    seq = lambda w: pl.BlockSpec((rows, w), lambda i: (tok0 // rows + i, 0))
    const = lambda shape: pl.BlockSpec(shape, lambda i: tuple(0 for _ in shape))
    once = lambda shape: pl.BlockSpec(shape, lambda i: (0, 0), pipeline_mode=pl.Buffered(1))
    st_spec = pl.BlockSpec((nseq, 1, H, DV, DK), lambda i: (i, 0, 0, 0, 0))
    in_specs = [seq(c0), seq(c0), seq(2 * c0), seq(H * DV), seq(H * DV), seq(2 * FW),
                const(gn.shape), once((l, l)), once((l, l))]
    out_specs = [pl.BlockSpec((rows, MIX0), lambda i: (i, 0))]
    out_shape = [jax.ShapeDtypeStruct((b * l, MIX0), BF)]
    args = [q, k, la, v, g, xcs, gn, cl, sl]
    if states is not None:
        in_specs += [st_spec, st_spec]
        args += list(states)
    else:
        out_specs += [st_spec, st_spec]
        out_shape += [jax.ShapeDtypeStruct((b, 1, H, DV, DK), F32)] * 2
    for w, layer in casts:
        _, r, c = w.shape
        assert r % steps == 0
        in_specs.append(pl.BlockSpec((1, r // steps, c), lambda i, layer=layer: (layer, i, 0)))
        args.append(w)
        out_specs.append(pl.BlockSpec((r // steps, c), lambda i: (i, 0)))
        out_shape.append(jax.ShapeDtypeStruct((r, c), BF))
    if mod is not None:
        mod_in, mod_out, mod_shape = _mod_specs(mod[1], mod[4], steps)
        in_specs += mod_in
        args += list(mod[:4])
        out_specs.append(mod_out)
        out_shape.append(mod_shape)
    return pl.pallas_call(
        functools.partial(_gla_kernel, l, nseq, states is not None, len(casts), mod is not None),
        grid=(steps,),
        in_specs=in_specs,
        out_specs=out_specs,
        out_shape=out_shape,
        scratch_shapes=[pltpu.VMEM((rows, 2 * DK), F32),
                        pltpu.VMEM((rows, 2 * DK), BF),
                        pltpu.VMEM((nseq * nchunks, 1, 2 * DK), F32),
                        pltpu.VMEM((nseq * nchunks, DV, 2 * DK), F32),
                        pltpu.VMEM((nseq * nchunks, DV, 2 * DK), BF),
                        pltpu.VMEM((DV, 2 * DK), F32)],
        compiler_params=pltpu.CompilerParams(
            dimension_semantics=("arbitrary",), vmem_limit_bytes=VMEM_LIMIT),
        name="gla_fnet",
    )(*args)


def _ffn(x1, mod, ng, w1_ref, w2_ref):
    h = _rms_mod(x1, ng, mod[4], mod[3]).astype(BF)
    wd = D_FF // FF_SPLIT
    out = None
    for c in range(FF_SPLIT):
        hid = jnp.maximum(_dot(h, w1_ref[:, c * wd:(c + 1) * wd]), 0.0)
        part = _dot((hid * hid).astype(BF), w2_ref[c * wd:(c + 1) * wd, :])
        out = part if out is None else out + part
    return x1 + mod[5] * out


def _ev_out_kernel(tl, xc_ref, xs_ref, mc_ref, ms_ref, mod_ref, ng_ref, wo_ref, w1_ref, w2_ref,
                   o_ref):
    s = pl.program_id(0)
    is_ctx = s < tl.n_ctx
    mod = tl.mod_vecs(s, mod_ref)
    x = jnp.where(is_ctx, xc_ref[...], xs_ref[...])
    mix = jnp.where(is_ctx, mc_ref[...], ms_ref[...])
    x1 = x + mod[2] * _dot(mix, wo_ref[...])
    o_ref[...] = _ffn(x1, mod, ng_ref[0], w1_ref, w2_ref)


def _ev_out(tl, xc, xs, mix_c, mix_s, mods, ngf, wo, w1, w2, layer):
    ntok = (tl.n_ctx + tl.n_smp) * TM
    return pl.pallas_call(
        functools.partial(_ev_out_kernel, tl),
        grid=(tl.steps,),
        in_specs=tl.specs(D, True) + tl.specs(MIX0, True) + [
            tl.mod_spec(), tl.whole(ngf.shape, layer),
            tl.weight(wo.shape), tl.weight(w1.shape), tl.weight(w2.shape)],
        out_specs=tl.specs(D, False)[0],
        out_shape=jax.ShapeDtypeStruct((ntok, D), F32),
        compiler_params=pltpu.CompilerParams(
            dimension_semantics=("arbitrary",), vmem_limit_bytes=VMEM_LIMIT),
        name="ev_out_ffn",
    )(xc, xs, mix_c, mix_s, mods, ngf, wo, w1, w2)


def _odd_kernel(tl, ctx_rows, smp_rows, x_ref, mod_ref, ngm_ref, ngf_ref, gfin_ref, ws_ref, gb_ref,
                cw_ref, wi_ref, wo_ref, w1_ref, w2_ref, yc_ref, ys_ref):
    s = pl.program_id(0)
    is_ctx = s < tl.n_ctx
    x = x_ref[...]
    mod = tl.mod_vecs(s, mod_ref)
    h = _rms_mod(x, ngm_ref[0], mod[1], mod[0]).astype(BF)
    v = _gelu_tanh(_dot(h, wi_ref[:, GW:2 * GW])).astype(BF)
    u = _gelu_tanh(_dot(h, wi_ref[:, 0:GW]))
    z = (_dot(h, wi_ref[:, 2 * GW + CW:2 * GW + 2 * CW])
         * _dot(h, wi_ref[:, 2 * GW + 2 * CW:2 * GW + 3 * CW]))
    gate_b = _dot(h, wi_ref[:, 2 * GW:2 * GW + CW])

    sp_rows = []
    for c in range(TM // GCH):
        cols = []
        for gi in range(GW // GCH):
            vg = v[c * GCH:(c + 1) * GCH, gi * GCH:(gi + 1) * GCH]
            cols.append(_dot(ws_ref[0, gi].astype(BF), vg))
        sp_rows.append(jnp.concatenate(cols, axis=1) + gb_ref[...])
    out_c = u * jnp.concatenate(sp_rows, axis=0)

    last = jnp.where(is_ctx, ctx_rows - 1, smp_rows - 1)
    pos = lax.broadcasted_iota(jnp.int32, (TM, CW), 0) & last
    zl = jnp.where(pos >= 1, pltpu.roll(z, 1, axis=0), 0.0)
    zr = jnp.where(pos < last, pltpu.roll(z, TM - 1, axis=0), 0.0)
    cw = cw_ref[0]
    out_d = gate_b * (zl * cw[0:1] + z * cw[1:2] + zr * cw[2:3])

    mix = jnp.concatenate([out_c, out_d], axis=1).astype(BF)
    x1 = x + mod[2] * _dot(mix, wo_ref[...])
    x2 = _ffn(x1, mod, ngf_ref[0], w1_ref, w2_ref)
    ms = jnp.mean(x2 * x2, axis=-1, keepdims=True)
    y = x2 * lax.rsqrt(ms + EPS) * gfin_ref[...]

    @pl.when(is_ctx)
    def _store_ctx():
        yc_ref[...] = y

    @pl.when(jnp.logical_not(is_ctx))
    def _store_smp():
        ys_ref[...] = y


def _odd(tl, ctx_rows, smp_rows, x, mods, ngm, ngf, gfin, gmlp_ws, gbias, conv_w, wi, wo, w1, w2,
         layer):
    for r in (ctx_rows, smp_rows):
        assert TM % r == 0 and r & (r - 1) == 0
    assert TM % GCH == 0
    return pl.pallas_call(
        functools.partial(_odd_kernel, tl, ctx_rows, smp_rows),
        grid=(tl.steps,),
        in_specs=tl.specs(D, False) + [
            tl.mod_spec(), tl.whole(ngm.shape, layer), tl.whole(ngf.shape, layer),
            tl.whole(gfin.shape), tl.whole(gmlp_ws.shape), tl.whole(gbias.shape),
            tl.whole(conv_w.shape),
            tl.weight(wi.shape), tl.weight(wo.shape), tl.weight(w1.shape), tl.weight(w2.shape)],
        out_specs=tl.specs(D, True),
        out_shape=[jax.ShapeDtypeStruct((tl.n_ctx * TM, D), F32),
                   jax.ShapeDtypeStruct((tl.n_smp * TM, D), F32)],
        compiler_params=pltpu.CompilerParams(
            dimension_semantics=("arbitrary",), vmem_limit_bytes=VMEM_LIMIT),
        name="odd_ffn_final",
    )(x, mods, ngm, ngf, gfin, gmlp_ws, gbias, conv_w, wi, wo, w1, w2)


def _dft_pos(l):
    n = np.arange(l)
    ang = 2.0 * np.pi * ((n[:, None] * n[None, :]) % l) / l
    to_bf = lambda a: jnp.asarray(a, F32).astype(BF)
    return to_bf(np.cos(ang) / np.sqrt(l)), to_bf(-np.sin(ang) / np.sqrt(l))


def _dft_chan():
    m = np.arange(FG)
    angc = 2.0 * np.pi * ((m[:, None] * m[None, :]) % FG) / FG
    eye = np.eye(FW // FG)
    cc = np.kron(eye, np.cos(angc)) / np.sqrt(FG)
    sc = np.kron(eye, np.sin(angc)) / np.sqrt(FG)
    return jnp.asarray(np.concatenate([cc, sc], axis=1), F32).astype(BF)


def kernel(x_prompt, x_sample, state_gla_fwd, state_gla_bwd, c, c_ctx, ada_w, ada_b, norm_mix_g,
           norm_ffn_g, ffn_w1, ffn_w2, ev_w_in, ev_w_out, gla_w2_f, gla_b2_f, gla_w2_b, gla_b2_b,
           gla_norm_g, od_w_in, od_w_out, gmlp_ws, gmlp_b, conv_w, final_norm_g):
    b_ctx, l_ctx, _ = x_prompt.shape
    b_smp, l_smp, _ = x_sample.shape
    depth = ada_w.shape[0]
    assert TM % l_ctx == 0 and l_smp % TM == 0 and b_smp % SUB == 0
    n_ctx, n_smp, tps = b_ctx * l_ctx // TM, b_smp * l_smp // TM, l_smp // TM
    tl = _Tiles(n_ctx, n_smp, tps)

    mod_args = (c_ctx.reshape(1, D), c, ada_w, ada_b.reshape(depth, 1, N_MOD * D))
    ngm = norm_mix_g.reshape(depth, 1, D)
    ngf = norm_ffn_g.reshape(depth, 1, D)
    gbias = jnp.repeat(gmlp_b[0].T, GCH, axis=1)
    xc = x_prompt.reshape(b_ctx * l_ctx, D)
    xs = x_sample.reshape(b_smp * l_smp, D)

    q, k, la, v, g, xcs, mods0 = _ev_in(
        _Tiles(n_ctx, n_smp, tps, lead=D // LANE), xc, xs, mod_args + (0,), ngm,
        jnp.swapaxes(ev_w_in, 1, 2), gla_w2_f, gla_w2_b, gla_b2_f, gla_b2_b, _dft_chan())
    mix_c, sf, sb, wo0, w1_0, w2_0 = _gla(
        b_ctx, l_ctx, CTX_SEQS, 0, q, k, la, v, g, xcs, gla_norm_g, *_dft_pos(l_ctx), None,
        [(ev_w_out, 0), (ffn_w1, 0), (ffn_w2, 0)])
    mix_s, wi1, wo1, w1_1, w2_1, mods1 = _gla(
        b_smp, l_smp, 1, b_ctx * l_ctx, q, k, la, v, g, xcs, gla_norm_g, *_dft_pos(l_smp),
        (jnp.swapaxes(state_gla_fwd, -1, -2), jnp.swapaxes(state_gla_bwd, -1, -2)),
        [(od_w_in, 0), (od_w_out, 0), (ffn_w1, 1), (ffn_w2, 1)], mod=mod_args + (1,))
    x1 = _ev_out(tl, xc, xs, mix_c, mix_s, mods0, ngf, wo0, w1_0, w2_0, 0)

    yc, ys = _odd(tl, l_ctx, GRID_W, x1, mods1, ngm, ngf, final_norm_g.reshape(1, D), gmlp_ws, gbias,
                  conv_w, wi1, wo1, w1_1, w2_1, 1)
    return (yc.reshape(b_ctx, l_ctx, D), ys.reshape(b_smp, l_smp, D),
            jnp.swapaxes(sf, -1, -2), jnp.swapaxes(sb, -1, -2))
```
